```python
import math
import jax, jax.numpy as jnp
from jax import lax
import numpy as np

D_MODEL = 4096
BATCH = 2
SEQ = 4096
DEPTH = 2

CHUNK = 64
EPS = 1e-6
N_EVEN = (DEPTH + 1) // 2
N_ODD = DEPTH // 2

POOL_WINDOWS = (2, 4, 8, 16)
POOL_WIDTH = D_MODEL // 2
POOL_GROUP = POOL_WIDTH // len(POOL_WINDOWS)
GDN_HEAD_DIM = 128
GDN_WIDTH = D_MODEL // 2
GDN_HEADS = GDN_WIDTH // GDN_HEAD_DIM
CONV_WIDTH = 4
AB_IN = POOL_WIDTH + 4 * GDN_WIDTH + 2 * GDN_HEADS
AB_MIX = POOL_WIDTH + GDN_WIDTH

SGU_WIDTH = D_MODEL
SGU_GROUPS = 16
SGU_GROUP = SGU_WIDTH // SGU_GROUPS
SGU_LEN = 128

D_FF = 256 * (-(-(8 * D_MODEL) // (3 * 256)))

kernel_name = 'hybrid_pool_gdn_sgu_streaming'


def rms_norm(x, g):
    xf = x.astype(jnp.float32)
    y = xf * lax.rsqrt(jnp.mean(xf * xf, axis=-1, keepdims=True) + EPS)
    return (y * g.astype(jnp.float32)).astype(x.dtype)


def layer_norm(x, g, b):
    xf = x.astype(jnp.float32)
    mu = jnp.mean(xf, axis=-1, keepdims=True)
    xc = xf - mu
    y = xc * lax.rsqrt(jnp.mean(xc * xc, axis=-1, keepdims=True) + EPS)
    return (y * g.astype(jnp.float32) + b.astype(jnp.float32)).astype(x.dtype)


def l2norm(x):
    return x * lax.rsqrt(jnp.sum(x * x, axis=-1, keepdims=True) + EPS)


def causal_dwconv(x, w):
    c = x.shape[-1]
    return lax.conv_general_dilated(
        x, w[:, None, :].astype(x.dtype), window_strides=(1,),
        padding=[(CONV_WIDTH - 1, 0)], dimension_numbers=('NWC', 'WIO', 'NWC'),
        feature_group_count=c)


def pool_mixer(x, w_grp, scale):
    b, s, _ = x.shape
    n_g = len(POOL_WINDOWS)
    xg = x.astype(jnp.float32).reshape(b, s, n_g, POOL_GROUP)
    cs = jnp.cumsum(xg, axis=1)
    cs = jnp.concatenate([jnp.zeros_like(cs[:, :1]), cs], axis=1)
    pos = jnp.arange(1, s + 1)
    outs = []
    for gi, win in enumerate(POOL_WINDOWS):
        c = cs[:, :, gi]
        lower = jnp.concatenate(
            [jnp.zeros((b, win - 1, POOL_GROUP), jnp.float32), c[:, :s - win + 1]], axis=1)
        cnt = jnp.minimum(pos, win).astype(jnp.float32)[None, :, None]
        outs.append((c[:, 1:] - lower) / cnt - xg[:, :, gi])
    y = jnp.stack(outs, axis=2).astype(x.dtype)
    y = jnp.einsum('bsgc,gcd->bsgd', y, w_grp)
    return (y.reshape(b, s, POOL_WIDTH) * scale).astype(x.dtype)


def gated_delta_rule(q, k, v, beta, g):
    b, s, h, d = q.shape
    n = s // CHUNK

    def chunks(t):
        return t.reshape(b, n, CHUNK, h, -1).transpose(0, 3, 1, 2, 4)

    q, k, v = chunks(q), chunks(k), chunks(v)
    beta = beta.reshape(b, n, CHUNK, h).transpose(0, 3, 1, 2)
    gc = jnp.cumsum(g.reshape(b, n, CHUNK, h).transpose(0, 3, 1, 2), axis=-1)
    causal = jnp.tril(jnp.ones((CHUNK, CHUNK), bool))
    strict = jnp.tril(jnp.ones((CHUNK, CHUNK), bool), -1)
    decay = jnp.exp(jnp.where(causal, gc[..., :, None] - gc[..., None, :], -jnp.inf))
    kk = jnp.einsum('bhnid,bhnjd->bhnij', k, k)
    lmat = jnp.where(strict, beta[..., :, None] * kk * decay, 0.0)
    eye = jnp.eye(CHUNK, dtype=jnp.float32)
    gamma = jnp.exp(gc)[..., None]
    rhs = jnp.concatenate([k * beta[..., None] * gamma, v * beta[..., None]], axis=-1)
    sol = lax.linalg.triangular_solve(eye + lmat, rhs, left_side=True, lower=True,
                                      unit_diagonal=True)
    w, u = sol[..., :d], sol[..., d:]
    attn = jnp.einsum('bhnid,bhnjd->bhnij', q, k) * decay
    q_dec = q * gamma
    k_dec = k * jnp.exp(gc[..., -1:] - gc)[..., None]
    g_last = jnp.exp(gc[..., -1])

    def step(state, inp):
        w_c, u_c, qd, kd, a, gl = inp
        v_new = u_c - jnp.einsum('bhcd,bhde->bhce', w_c, state)
        o = jnp.einsum('bhcd,bhde->bhce', qd, state) + jnp.einsum('bhcj,bhje->bhce', a, v_new)
        state = gl[..., None, None] * state + jnp.einsum('bhcd,bhce->bhde', kd, v_new)
        return state, o

    xs = tuple(jnp.moveaxis(t, 2, 0) for t in (w, u, q_dec, k_dec, attn, g_last))
    s0 = jnp.zeros((b, h, d, v.shape[-1]), jnp.float32)
    _, o = lax.scan(step, s0, xs)
    return o.transpose(1, 0, 3, 2, 4).reshape(b, s, h, -1)


def mix_pool_gdn(h, w_in, pool_w, pool_scale, conv_w, a_log, dt_bias, norm_w, w_out):
    b, s, _ = h.shape
    f32 = jnp.float32
    z = h @ w_in
    xa = z[..., :POOL_WIDTH]
    qkv = z[..., POOL_WIDTH:POOL_WIDTH + 3 * GDN_WIDTH]
    gate = z[..., POOL_WIDTH + 3 * GDN_WIDTH:POOL_WIDTH + 4 * GDN_WIDTH]
    beta_logit = z[..., AB_IN - 2 * GDN_HEADS:AB_IN - GDN_HEADS]
    dec_logit = z[..., AB_IN - GDN_HEADS:]
    ya = pool_mixer(xa, pool_w, pool_scale)
    qkv = jax.nn.silu(causal_dwconv(qkv, conv_w)).astype(f32)
    q, k, v = [t.reshape(b, s, GDN_HEADS, GDN_HEAD_DIM) for t in jnp.split(qkv, 3, axis=-1)]
    q = l2norm(q) * (GDN_HEAD_DIM ** -0.5)
    k = l2norm(k)
    beta = jax.nn.sigmoid(beta_logit.astype(f32))
    g = -jnp.exp(a_log.astype(f32)) * jax.nn.softplus(dec_logit.astype(f32) + dt_bias.astype(f32))
    o = gated_delta_rule(q, k, v, beta, g)
    o = rms_norm(o, norm_w) * jax.nn.silu(gate.astype(f32).reshape(b, s, GDN_HEADS, GDN_HEAD_DIM))
    yb = o.reshape(b, s, GDN_WIDTH).astype(h.dtype)
    return jnp.concatenate([ya, yb], axis=-1) @ w_out


def mix_sgu(h, w_in, ln_g, ln_b, w_s, b_s, w_out):
    b, s, _ = h.shape
    z = jax.nn.gelu(h @ w_in)
    u, v = z[..., :SGU_WIDTH], z[..., SGU_WIDTH:]
    v = layer_norm(v, ln_g, ln_b)
    n = s // SGU_LEN
    vg = v.reshape(b, n, SGU_LEN, SGU_GROUPS, SGU_GROUP)
    pos = jnp.arange(SGU_LEN)
    mask = (pos[None, :] // CHUNK) <= (pos[:, None] // CHUNK)
    ws = jnp.where(mask[None], w_s, jnp.zeros_like(w_s))
    sv = jnp.einsum('gij,bnjgc->bnigc', ws, vg) + b_s.T[:, :, None]
    return (u * sv.reshape(b, s, SGU_WIDTH)) @ w_out


def swiglu(h, w_gate, w_up, w_down):
    return (jax.nn.silu(h @ w_gate) * (h @ w_up)) @ w_down


def setup_inputs(seed: int = 0) -> dict:
    key = jax.random.key(seed)
    ks = jax.random.split(key, 24)
    f32 = jnp.float32

    def nrm(k, shape, scale):
        return jax.random.normal(k, shape, f32) * scale

    def gain(k, shape):
        return 1.0 + 0.02 * jax.random.normal(k, shape, f32)

    dt = jnp.exp(jax.random.uniform(ks[7], (N_EVEN, GDN_HEADS), f32,
                                    math.log(1e-3), math.log(1e-1)))
    return {
        'x': jax.random.normal(ks[0], (BATCH, SEQ, D_MODEL), f32),
        'norm_mix_pre': gain(ks[1], (DEPTH, D_MODEL)),
        'norm_mix_post': gain(ks[2], (DEPTH, D_MODEL)),
        'norm_ffn_pre': gain(ks[3], (DEPTH, D_MODEL)),
        'norm_ffn_post': gain(ks[4], (DEPTH, D_MODEL)),
        'ab_w_in': nrm(ks[5], (N_EVEN, D_MODEL, AB_IN), D_MODEL ** -0.5),
        'pool_w': nrm(ks[6], (N_EVEN, len(POOL_WINDOWS), POOL_GROUP, POOL_GROUP), POOL_GROUP ** -0.5),
        'pool_scale': gain(ks[8], (N_EVEN, POOL_WIDTH)),
        'gdn_conv': nrm(ks[9], (N_EVEN, CONV_WIDTH, 3 * GDN_WIDTH), CONV_WIDTH ** -0.5),
        'gdn_a_log': jnp.log(jax.random.uniform(ks[10], (N_EVEN, GDN_HEADS), f32, 1.0, 16.0)),
        'gdn_dt_bias': jnp.log(jnp.expm1(dt)),
        'gdn_norm': gain(ks[11], (N_EVEN, GDN_HEAD_DIM)),
        'ab_w_out': nrm(ks[12], (N_EVEN, AB_MIX, D_MODEL), AB_MIX ** -0.5),
        'sgu_w_in': nrm(ks[13], (N_ODD, D_MODEL, 2 * SGU_WIDTH), D_MODEL ** -0.5),
        'sgu_ln_g': gain(ks[14], (N_ODD, SGU_WIDTH)),
        'sgu_ln_b': nrm(ks[15], (N_ODD, SGU_WIDTH), 0.02),
        'sgu_w_s': nrm(ks[16], (N_ODD, SGU_GROUPS, SGU_LEN, SGU_LEN), SGU_LEN ** -0.5),
        'sgu_b_s': gain(ks[17], (N_ODD, SGU_GROUPS, SGU_LEN)),
        'sgu_w_out': nrm(ks[18], (N_ODD, SGU_WIDTH, D_MODEL), SGU_WIDTH ** -0.5),
        'ffn_w_gate': nrm(ks[19], (DEPTH, D_MODEL, D_FF), D_MODEL ** -0.5),
        'ffn_w_up': nrm(ks[20], (DEPTH, D_MODEL, D_FF), D_MODEL ** -0.5),
        'ffn_w_down': nrm(ks[21], (DEPTH, D_FF, D_MODEL), D_FF ** -0.5),
    }


def reference(x, norm_mix_pre, norm_mix_post, norm_ffn_pre, norm_ffn_post,
              ab_w_in, pool_w, pool_scale, gdn_conv, gdn_a_log, gdn_dt_bias, gdn_norm,
              ab_w_out, sgu_w_in, sgu_ln_g, sgu_ln_b, sgu_w_s, sgu_b_s, sgu_w_out,
              ffn_w_gate, ffn_w_up, ffn_w_down):
    for layer in range(DEPTH):
        i = layer // 2
        h = rms_norm(x, norm_mix_pre[layer])
        if layer % 2 == 0:
            y = mix_pool_gdn(h, ab_w_in[i], pool_w[i], pool_scale[i], gdn_conv[i],
                             gdn_a_log[i], gdn_dt_bias[i], gdn_norm[i], ab_w_out[i])
        else:
            y = mix_sgu(h, sgu_w_in[i], sgu_ln_g[i], sgu_ln_b[i], sgu_w_s[i],
                        sgu_b_s[i], sgu_w_out[i])
        x = x + rms_norm(y, norm_mix_post[layer])
        h = rms_norm(x, norm_ffn_pre[layer])
        y = swiglu(h, ffn_w_gate[layer], ffn_w_up[layer], ffn_w_down[layer])
        x = x + rms_norm(y, norm_ffn_post[layer])
    return x
```

```python
import functools

import jax
import jax.numpy as jnp
from jax import lax
from jax.experimental import pallas as pl
from jax.experimental.pallas import tpu as pltpu

F32 = jnp.float32
BF16 = jnp.bfloat16

EPS = 1e-6
POOL_WINDOWS = (2, 4, 8, 16)
POOL_HALO = 16
GDN_HEAD_DIM = 128
CONV_WIDTH = 4
GDN_BLOCK = 128
GDN_HEADS_PER_STEP = 4
SGU_GROUPS = 16
SGU_LEN = 128
SGU_CHUNK = 64
LANES = 128
VMEM_LIMIT_BYTES = 56 * 1024 * 1024


def _params(*semantics):
    return pltpu.CompilerParams(dimension_semantics=semantics,
                                vmem_limit_bytes=VMEM_LIMIT_BYTES)


def _sigmoid(x):
    return 1.0 / (1.0 + jnp.exp(-x))


def _silu(x):
    return x * _sigmoid(x)


def _gelu_tanh(x):
    c = 0.7978845608028654
    return 0.5 * x * (1.0 + jnp.tanh(c * (x + 0.044715 * (x * x * x))))


def _rms(x, g):
    return x * lax.rsqrt(jnp.mean(x * x, axis=-1, keepdims=True) + EPS) * g


def _prenorm_kernel(x_ref, g_ref, h_ref):
    h_ref[...] = _rms(x_ref[...], g_ref[...]).astype(h_ref.dtype)


def prenorm(x, g, *, tr=256):
    t, d = x.shape
    return pl.pallas_call(
        _prenorm_kernel,
        grid=(t // tr,),
        in_specs=[pl.BlockSpec((tr, d), lambda i: (i, 0)),
                  pl.BlockSpec((1, d), lambda i: (0, 0))],
        out_specs=pl.BlockSpec((tr, d), lambda i: (i, 0)),
        out_shape=jax.ShapeDtypeStruct((t, d), BF16),
        compiler_params=_params("parallel"),
    )(x, g.reshape(1, d))


def _postnorm_kernel(y_ref, x_ref, gpost_ref, *rest, with_next):
    xn = x_ref[...] + _rms(y_ref[...], gpost_ref[...])
    if with_next:
        gpre_ref, xo_ref, h_ref = rest
        h_ref[...] = _rms(xn, gpre_ref[...]).astype(h_ref.dtype)
    else:
        (xo_ref,) = rest
    xo_ref[...] = xn


def postnorm_residual(y, x, g_post, g_pre_next=None, *, tr=256):
    t, d = x.shape
    with_next = g_pre_next is not None
    row = pl.BlockSpec((tr, d), lambda i: (i, 0))
    vec = pl.BlockSpec((1, d), lambda i: (0, 0))
    in_specs = [row, row, vec]
    args = [y, x, g_post.reshape(1, d)]
    out_specs = [row]
    out_shape = [jax.ShapeDtypeStruct((t, d), F32)]
    if with_next:
        in_specs.append(vec)
        args.append(g_pre_next.reshape(1, d))
        out_specs.append(row)
        out_shape.append(jax.ShapeDtypeStruct((t, d), BF16))
    out = pl.pallas_call(
        functools.partial(_postnorm_kernel, with_next=with_next),
        grid=(t // tr,),
        in_specs=in_specs,
        out_specs=out_specs,
        out_shape=out_shape,
        compiler_params=_params("parallel"),
    )(*args)
    return (out[0], out[1]) if with_next else (out[0], None)


def _mm_kernel(a_ref, w_ref, o_ref, *scratch, nk, act):
    def finish(r):
        if act == "gelu":
            r = _gelu_tanh(r)
        o_ref[...] = r.astype(o_ref.dtype)

    part = jnp.dot(a_ref[...], w_ref[...], preferred_element_type=F32)
    if nk == 1:
        finish(part)
        return
    (acc_ref,) = scratch
    k = pl.program_id(2)

    @pl.when(k == 0)
    def _():
        acc_ref[...] = part

    @pl.when(k > 0)
    def _():
        acc_ref[...] += part

    @pl.when(k == nk - 1)
    def _():
        finish(acc_ref[...])


def matmul(a, w, *, tm, tn, tk=None, out_dtype=F32, act=None):
    m, kdim = a.shape
    _, n = w.shape
    tk = kdim if tk is None else tk
    nk = kdim // tk
    assert m % tm == 0 and n % tn == 0 and kdim % tk == 0
    scratch = [] if nk == 1 else [pltpu.VMEM((tm, tn), F32)]
    return pl.pallas_call(
        functools.partial(_mm_kernel, nk=nk, act=act),
        grid=(m // tm, n // tn, nk),
        in_specs=[pl.BlockSpec((tm, tk), lambda i, j, k: (i, k)),
                  pl.BlockSpec((tk, tn), lambda i, j, k: (k, j))],
        out_specs=pl.BlockSpec((tm, tn), lambda i, j, k: (i, j)),
        out_shape=jax.ShapeDtypeStruct((m, n), out_dtype),
        scratch_shapes=scratch,
        compiler_params=_params("parallel", "parallel", "arbitrary"),
    )(a, w)


def _mm_nt_kernel(w_ref, a_ref, o_ref):
    o_ref[...] = lax.dot_general(w_ref[...], a_ref[...], (((1,), (1,)), ((), ())),
                                 preferred_element_type=F32)


def matmul_nt(wt, a, *, tn):
    r, kdim = wt.shape
    m, _ = a.shape
    return pl.pallas_call(
        _mm_nt_kernel,
        grid=(m // tn,),
        in_specs=[pl.BlockSpec((r, kdim), lambda i: (0, 0)),
                  pl.BlockSpec((tn, kdim), lambda i: (i, 0))],
        out_specs=pl.BlockSpec((r, tn), lambda i: (0, i)),
        out_shape=jax.ShapeDtypeStruct((r, m), F32),
        compiler_params=_params("parallel"),
    )(wt, a)


def _ffn_up_kernel(a_ref, wg_ref, wu_ref, o_ref, *, n_valid, tn):
    a = a_ref[...]
    g = jnp.dot(a, wg_ref[...], preferred_element_type=F32)
    u = jnp.dot(a, wu_ref[...], preferred_element_type=F32)
    col = pl.program_id(1) * tn + lax.broadcasted_iota(jnp.int32, g.shape, 1)
    o_ref[...] = jnp.where(col < n_valid, _silu(g) * u, 0.0).astype(o_ref.dtype)


def ffn_up(a, wg, wu, *, tm, tn):
    m, kdim = a.shape
    _, n = wg.shape
    nb = pl.cdiv(n, tn)
    return pl.pallas_call(
        functools.partial(_ffn_up_kernel, n_valid=n, tn=tn),
        grid=(m // tm, nb),
        in_specs=[pl.BlockSpec((tm, kdim), lambda i, j: (i, 0)),
                  pl.BlockSpec((kdim, tn), lambda i, j: (0, j)),
                  pl.BlockSpec((kdim, tn), lambda i, j: (0, j))],
        out_specs=pl.BlockSpec((tm, tn), lambda i, j: (i, j)),
        out_shape=jax.ShapeDtypeStruct((m, nb * tn), BF16),
        compiler_params=_params("parallel", "parallel"),
    )(a, wg, wu)


def _pool_kernel(halo_ref, x_ref, w_ref, scale_ref, o_ref, *, tb, seq, group):
    start = (pl.program_id(0) * tb) % seq
    keep_halo = (start > 0).astype(F32)
    pos = start + 1 + lax.broadcasted_iota(jnp.int32, (tb, 1), 0)
    for gi, win in enumerate(POOL_WINDOWS):
        cols = slice(gi * group, (gi + 1) * group)
        x = x_ref[:, cols]
        s = jnp.concatenate([halo_ref[:, cols] * keep_halo, x], axis=0)
        shift = 1
        while shift < win:
            s = s + pltpu.roll(s, shift, axis=0)
            shift *= 2
        cnt = jnp.minimum(pos, win).astype(F32)
        y = s[POOL_HALO:, :] / cnt - x
        r = jnp.dot(y.astype(BF16), w_ref[gi], preferred_element_type=F32)
        o_ref[:, cols] = (r * scale_ref[:, cols]).astype(o_ref.dtype)


def pool_mixer(z, w_grp, scale, *, seq, tb=256):
    t = z.shape[0]
    n_g, group, _ = w_grp.shape
    width = n_g * group
    hb = tb // POOL_HALO
    return pl.pallas_call(
        functools.partial(_pool_kernel, tb=tb, seq=seq, group=group),
        grid=(t // tb,),
        in_specs=[pl.BlockSpec((POOL_HALO, width), lambda i: (jnp.maximum(i * hb - 1, 0), 0)),
                  pl.BlockSpec((tb, width), lambda i: (i, 0)),
                  pl.BlockSpec((n_g, group, group), lambda i: (0, 0, 0)),
                  pl.BlockSpec((1, width), lambda i: (0, 0))],
        out_specs=pl.BlockSpec((tb, width), lambda i: (i, 0)),
        out_shape=jax.ShapeDtypeStruct((t, width), BF16),
        compiler_params=_params("parallel"),
    )(z, z, w_grp, scale.reshape(1, width))


def _gdn_kernel(q_ref, k_ref, v_ref, gate_ref, tail_ref, cq_ref, ck_ref, cv_ref,
                alog_ref, dtb_ref, nw_ref, o_ref, prev_ref, state_ref, *, hps, n_heads):
    c = GDN_BLOCK
    d = GDN_HEAD_DIM
    t = pl.program_id(2)
    hg = pl.program_id(1)

    @pl.when(t == 0)
    def _():
        prev_ref[...] = jnp.zeros_like(prev_ref)
        state_ref[...] = jnp.zeros_like(state_ref)

    row_w = lax.broadcasted_iota(jnp.int32, (c, hps * d), 0)

    def conv_silu(x_ref, idx, cw_ref):
        cur = x_ref[...]
        prev = prev_ref[idx]
        w = cw_ref[...]
        acc = cur * w[CONV_WIDTH - 1:CONV_WIDTH, :]
        for back in range(1, CONV_WIDTH):
            comb = jnp.where(row_w >= c - back, prev, cur)
            acc = acc + pltpu.roll(comb, back, axis=0) * w[CONV_WIDTH - 1 - back:CONV_WIDTH - back, :]
        prev_ref[idx] = cur
        return _silu(acc)

    qs = conv_silu(q_ref, 0, cq_ref)
    ks = conv_silu(k_ref, 1, ck_ref)
    vs = conv_silu(v_ref, 2, cv_ref)

    ii = lax.broadcasted_iota(jnp.int32, (c, c), 0)
    jj = lax.broadcasted_iota(jnp.int32, (c, c), 1)
    causal = ii >= jj
    strict = ii > jj

    def dot(a, b):
        return jnp.dot(a.astype(BF16), b.astype(BF16), preferred_element_type=F32)

    def dot_nt(a, b):
        return lax.dot_general(a.astype(BF16), b.astype(BF16), (((1,), (1,)), ((), ())),
                               preferred_element_type=F32)

    for j in range(hps):
        cols = slice(j * d, (j + 1) * d)
        head = hg * hps + j
        q = qs[:, cols]
        k = ks[:, cols]
        v = vs[:, cols]
        q = q * lax.rsqrt(jnp.sum(q * q, axis=-1, keepdims=True) + EPS) * (d ** -0.5)
        k = k * lax.rsqrt(jnp.sum(k * k, axis=-1, keepdims=True) + EPS)

        beta_row = _sigmoid(tail_ref[pl.ds(head, 1), :])
        dl = tail_ref[pl.ds(n_heads + head, 1), :] + dtb_ref[j]
        softplus = jnp.maximum(dl, 0.0) + jnp.log1p(jnp.exp(-jnp.abs(dl)))
        g_row = -jnp.exp(alog_ref[j]) * softplus
        beta = jnp.broadcast_to(beta_row, (c, c)).T
        gcol = jnp.broadcast_to(g_row, (c, c)).T
        shift = 1
        gc = gcol
        while shift < c:
            gc = gc + jnp.where(ii >= shift, pltpu.roll(gc, shift, axis=0), 0.0)
            shift *= 2
        gc_t = gc.T
        decay = jnp.exp(jnp.where(causal, gc - gc_t, -jnp.inf))
        gamma = jnp.exp(gc)
        gc_last = gc[c - 1:c, :]
        kb = k * beta
        lmat = jnp.where(strict, dot_nt(kb, k) * decay, 0.0)

        n = -lmat
        p = lmat
        width = 2
        while width < c:
            p = dot(p, p)
            n = n + p + dot(n, p)
            width *= 2
        rhs = jnp.concatenate([kb * gamma, v * beta], axis=1)
        sol = rhs + dot(n, rhs)
        w_c = sol[:, :d]
        u_c = sol[:, d:]
        attn = dot_nt(q, k) * decay
        qd = q * gamma
        kd = k * jnp.exp(gc_last - gc)

        state = state_ref[j]
        v_new = u_c - dot(w_c, state)
        o = dot(qd, state) + dot(attn, v_new)
        state_ref[j] = jnp.exp(gc_last) * state + dot(kd.T, v_new)

        gate = gate_ref[:, cols]
        o_ref[:, cols] = (_rms(o, nw_ref[...]) * _silu(gate)).astype(o_ref.dtype)


def gated_deltanet(z, tail_t, conv_w, a_log, dt_bias, norm_w, *, batch, seq, col0, width):
    d = GDN_HEAD_DIM
    c = GDN_BLOCK
    hps = GDN_HEADS_PER_STEP
    n_heads = width // d
    bw = hps * d
    nblk = seq // c
    cb0 = col0 // bw
    wb = width // bw

    def zspec(part):
        return pl.BlockSpec((c, bw), lambda b, h, t: (b * nblk + t, cb0 + part * wb + h))

    def cspec(part):
        return pl.BlockSpec((CONV_WIDTH, bw), lambda b, h, t: (0, part * wb + h))

    lane_rep = lambda a: jnp.broadcast_to(a.astype(F32)[:, None, None], (n_heads, 1, c))
    hspec = pl.BlockSpec((hps, 1, c), lambda b, h, t: (h, 0, 0))
    return pl.pallas_call(
        functools.partial(_gdn_kernel, hps=hps, n_heads=n_heads),
        grid=(batch, n_heads // hps, nblk),
        in_specs=[zspec(0), zspec(1), zspec(2), zspec(3),
                  pl.BlockSpec((2 * n_heads, c), lambda b, h, t: (0, b * nblk + t)),
                  cspec(0), cspec(1), cspec(2), hspec, hspec,
                  pl.BlockSpec((1, d), lambda b, h, t: (0, 0))],
        out_specs=pl.BlockSpec((c, bw), lambda b, h, t: (b * nblk + t, h)),
        out_shape=jax.ShapeDtypeStruct((batch * seq, width), BF16),
        scratch_shapes=[pltpu.VMEM((3, c, bw), F32), pltpu.VMEM((hps, d, d), F32)],
        compiler_params=_params("parallel", "parallel", "arbitrary"),
    )(z, z, z, z, tail_t, conv_w, conv_w, conv_w, lane_rep(a_log), lane_rep(dt_bias),
      norm_w.reshape(1, d))


def _sgu_kernel(u_ref, v_ref, lng_ref, lnb_ref, ws_ref, bs_ref, o_ref, *, group):
    v = v_ref[...]
    mu = jnp.mean(v, axis=-1, keepdims=True)
    xc = v - mu
    vn = xc * lax.rsqrt(jnp.mean(xc * xc, axis=-1, keepdims=True) + EPS) * lng_ref[...] + lnb_ref[...]
    vb = vn.astype(BF16)
    ii = lax.broadcasted_iota(jnp.int32, (SGU_LEN, SGU_LEN), 0)
    jj = lax.broadcasted_iota(jnp.int32, (SGU_LEN, SGU_LEN), 1)
    chunk_bits = SGU_CHUNK.bit_length() - 1
    mask = (jj >> chunk_bits) <= (ii >> chunk_bits)
    for g in range(SGU_GROUPS):
        cols = slice(g * group, (g + 1) * group)
        ws = jnp.where(mask, ws_ref[g], 0.0).astype(BF16)
        sv = jnp.dot(ws, vb[:, cols], preferred_element_type=F32) + bs_ref[:, g:g + 1]
        o_ref[:, cols] = (u_ref[:, cols] * sv).astype(o_ref.dtype)


def spatial_gating(z, ln_g, ln_b, w_s, b_s):
    t, w2 = z.shape
    width = w2 // 2
    group = width // SGU_GROUPS
    vec = pl.BlockSpec((1, width), lambda i: (0, 0))
    return pl.pallas_call(
        functools.partial(_sgu_kernel, group=group),
        grid=(t // SGU_LEN,),
        in_specs=[pl.BlockSpec((SGU_LEN, width), lambda i: (i, 0)),
                  pl.BlockSpec((SGU_LEN, width), lambda i: (i, 1)),
                  vec, vec,
                  pl.BlockSpec((SGU_GROUPS, SGU_LEN, SGU_LEN), lambda i: (0, 0, 0)),
                  pl.BlockSpec((SGU_LEN, SGU_GROUPS), lambda i: (0, 0))],
        out_specs=pl.BlockSpec((SGU_LEN, width), lambda i: (i, 0)),
        out_shape=jax.ShapeDtypeStruct((t, width), BF16),
        compiler_params=_params("parallel"),
    )(z, z, ln_g.reshape(1, width), ln_b.reshape(1, width), w_s, b_s.T)


def _mix_pool_gdn(h, w_in, pool_w, pool_scale, conv_w, a_log, dt_bias, norm_w, w_out, *, batch, seq):
    n_g, group, _ = pool_w.shape
    pool_width = n_g * group
    n_heads = a_log.shape[0]
    gdn_width = n_heads * GDN_HEAD_DIM
    main = pool_width + 4 * gdn_width
    z = matmul(h, w_in[:, :main].astype(BF16), tm=1024, tn=1024)
    tail_t = matmul_nt(w_in[:, main:].T.astype(BF16), h, tn=1024)
    ya = pool_mixer(z, pool_w.astype(BF16), pool_scale, seq=seq)
    yb = gated_deltanet(z, tail_t, conv_w, a_log, dt_bias, norm_w,
                        batch=batch, seq=seq, col0=pool_width, width=gdn_width)
    return matmul(jnp.concatenate([ya, yb], axis=-1), w_out.astype(BF16), tm=1024, tn=1024)


def _mix_sgu(h, w_in, ln_g, ln_b, w_s, b_s, w_out):
    z = matmul(h, w_in.astype(BF16), tm=1024, tn=1024, act="gelu")
    s = spatial_gating(z, ln_g, ln_b, w_s, b_s)
    return matmul(s, w_out.astype(BF16), tm=1024, tn=1024)


def _swiglu(h, w_gate, w_up, w_down, *, tn=512, tk=2816):
    act = ffn_up(h, w_gate.astype(BF16), w_up.astype(BF16), tm=1024, tn=tn)
    pad = act.shape[1] - w_down.shape[0]
    wd = jnp.pad(w_down.astype(BF16), ((0, pad), (0, 0)))
    return matmul(act, wd, tm=1024, tn=1024, tk=tk)


def kernel(x, norm_mix_pre, norm_mix_post, norm_ffn_pre, norm_ffn_post, ab_w_in, pool_w, pool_scale, gdn_conv, gdn_a_log, gdn_dt_bias, gdn_norm, ab_w_out, sgu_w_in, sgu_ln_g, sgu_ln_b, sgu_w_s, sgu_b_s, sgu_w_out, ffn_w_gate, ffn_w_up, ffn_w_down):
    batch, seq, d_model = x.shape
    depth = norm_mix_pre.shape[0]
    xt = x.reshape(batch * seq, d_model)
    h = prenorm(xt, norm_mix_pre[0])
    for layer in range(depth):
        i = layer // 2
        if layer % 2 == 0:
            y = _mix_pool_gdn(h, ab_w_in[i], pool_w[i], pool_scale[i], gdn_conv[i], gdn_a_log[i],
                              gdn_dt_bias[i], gdn_norm[i], ab_w_out[i], batch=batch, seq=seq)
        else:
            y = _mix_sgu(h, sgu_w_in[i], sgu_ln_g[i], sgu_ln_b[i], sgu_w_s[i], sgu_b_s[i],
                         sgu_w_out[i])
        xt, h = postnorm_residual(y, xt, norm_mix_post[layer], norm_ffn_pre[layer])
        y = _swiglu(h, ffn_w_gate[layer], ffn_w_up[layer], ffn_w_down[layer])
        nxt = norm_mix_pre[layer + 1] if layer + 1 < depth else None
        xt, h = postnorm_residual(y, xt, norm_ffn_post[layer], nxt)
    return xt.reshape(batch, seq, d_model)
```

```python
import functools

import jax
import jax.numpy as jnp
from jax import lax
from jax.experimental import pallas as pl
from jax.experimental.pallas import tpu as pltpu

F32 = jnp.float32
BF16 = jnp.bfloat16

EPS = 1e-6
POOL_WINDOWS = (2, 4, 8, 16)
POOL_HALO = 16
GDN_HEAD_DIM = 128
CONV_WIDTH = 4
GDN_BLOCK = 128
GDN_HEADS_PER_STEP = 16
SGU_GROUPS = 16
SGU_LEN = 128
SGU_CHUNK = 64
LANES = 128
SUBLANES = 8
VMEM_LIMIT_BYTES = 56 * 1024 * 1024


def _params(*semantics):
    return pltpu.CompilerParams(dimension_semantics=semantics,
                                vmem_limit_bytes=VMEM_LIMIT_BYTES)


def _sigmoid(x):
    return 1.0 / (1.0 + jnp.exp(-x))


def _silu(x):
    return x * _sigmoid(x)


def _gelu_tanh(x):
    c = 0.7978845608028654
    return 0.5 * x * (1.0 + jnp.tanh(c * (x + 0.044715 * (x * x * x))))


def _rms(x, g):
    return x * lax.rsqrt(jnp.mean(x * x, axis=-1, keepdims=True) + EPS) * g


def _prenorm_kernel(x_ref, g_ref, h_ref):
    h_ref[...] = _rms(x_ref[...], g_ref[...]).astype(h_ref.dtype)


def prenorm(x, g, *, tr=256):
    t, d = x.shape
    return pl.pallas_call(
        _prenorm_kernel,
        grid=(t // tr,),
        in_specs=[pl.BlockSpec((tr, d), lambda i: (i, 0)),
                  pl.BlockSpec((1, d), lambda i: (0, 0))],
        out_specs=pl.BlockSpec((tr, d), lambda i: (i, 0)),
        out_shape=jax.ShapeDtypeStruct((t, d), BF16),
        compiler_params=_params("parallel"),
    )(x, g.reshape(1, d))


def _postnorm_kernel(y_ref, x_ref, gpost_ref, *rest, with_next):
    xn = x_ref[...] + _rms(y_ref[...], gpost_ref[...])
    if with_next:
        gpre_ref, xo_ref, h_ref = rest
        h_ref[...] = _rms(xn, gpre_ref[...]).astype(h_ref.dtype)
    else:
        (xo_ref,) = rest
    xo_ref[...] = xn


def postnorm_residual(y, x, g_post, g_pre_next=None, *, tr=256):
    t, d = x.shape
    with_next = g_pre_next is not None
    row = pl.BlockSpec((tr, d), lambda i: (i, 0))
    vec = pl.BlockSpec((1, d), lambda i: (0, 0))
    in_specs = [row, row, vec]
    args = [y, x, g_post.reshape(1, d)]
    out_specs = [row]
    out_shape = [jax.ShapeDtypeStruct((t, d), F32)]
    if with_next:
        in_specs.append(vec)
        args.append(g_pre_next.reshape(1, d))
        out_specs.append(row)
        out_shape.append(jax.ShapeDtypeStruct((t, d), BF16))
    out = pl.pallas_call(
        functools.partial(_postnorm_kernel, with_next=with_next),
        grid=(t // tr,),
        in_specs=in_specs,
        out_specs=out_specs,
        out_shape=out_shape,
        compiler_params=_params("parallel"),
    )(*args)
    return (out[0], out[1]) if with_next else (out[0], None)


def _mm_kernel(a_ref, w_ref, o_ref, *scratch, nk, act):
    def finish(r):
        if act == "gelu":
            r = _gelu_tanh(r)
        o_ref[...] = r.astype(o_ref.dtype)

    part = jnp.dot(a_ref[...], w_ref[...], preferred_element_type=F32)
    if nk == 1:
        finish(part)
        return
    (acc_ref,) = scratch
    k = pl.program_id(2)

    @pl.when(k == 0)
    def _():
        acc_ref[...] = part

    @pl.when(k > 0)
    def _():
        acc_ref[...] += part

    @pl.when(k == nk - 1)
    def _():
        finish(acc_ref[...])


def matmul(a, w, *, tm, tn, tk=None, out_dtype=F32, act=None):
    m, kdim = a.shape
    _, n = w.shape
    tk = kdim if tk is None else tk
    nk = kdim // tk
    assert m % tm == 0 and n % tn == 0 and kdim % tk == 0
    scratch = [] if nk == 1 else [pltpu.VMEM((tm, tn), F32)]
    return pl.pallas_call(
        functools.partial(_mm_kernel, nk=nk, act=act),
        grid=(m // tm, n // tn, nk),
        in_specs=[pl.BlockSpec((tm, tk), lambda i, j, k: (i, k)),
                  pl.BlockSpec((tk, tn), lambda i, j, k: (k, j))],
        out_specs=pl.BlockSpec((tm, tn), lambda i, j, k: (i, j)),
        out_shape=jax.ShapeDtypeStruct((m, n), out_dtype),
        scratch_shapes=scratch,
        compiler_params=_params("parallel", "parallel", "arbitrary"),
    )(a, w)


def _gdn_gates_kernel(w_ref, a_ref, alog_ref, dtb_ref, o_ref, *, n_heads):
    logits = lax.dot_general(w_ref[...], a_ref[...], (((1,), (1,)), ((), ())),
                             preferred_element_type=F32)
    o_ref[:n_heads, :] = _sigmoid(logits[:n_heads, :])
    dl = logits[n_heads:, :] + dtb_ref[...]
    softplus = jnp.maximum(dl, 0.0) + jnp.log1p(jnp.exp(-jnp.abs(dl)))
    s = -jnp.exp(alog_ref[...]) * softplus
    pos = lax.broadcasted_iota(jnp.int32, s.shape, 1) & (GDN_BLOCK - 1)
    shift = 1
    while shift < GDN_BLOCK:
        s = s + jnp.where(pos >= shift, pltpu.roll(s, shift, axis=1), 0.0)
        shift *= 2
    o_ref[n_heads:, :] = s


def gdn_gates(wt, a, a_log, dt_bias, *, tn):
    r, kdim = wt.shape
    n_heads = r // 2
    m, _ = a.shape
    col = pl.BlockSpec((n_heads, 1), lambda i: (0, 0))
    return pl.pallas_call(
        functools.partial(_gdn_gates_kernel, n_heads=n_heads),
        grid=(m // tn,),
        in_specs=[pl.BlockSpec((r, kdim), lambda i: (0, 0)),
                  pl.BlockSpec((tn, kdim), lambda i: (i, 0)), col, col],
        out_specs=pl.BlockSpec((r, tn), lambda i: (0, i)),
        out_shape=jax.ShapeDtypeStruct((r, m), F32),
        compiler_params=_params("parallel"),
    )(wt, a, a_log.astype(F32).reshape(n_heads, 1), dt_bias.astype(F32).reshape(n_heads, 1))


def _ffn_up_kernel(a_ref, wg_ref, wu_ref, o_ref, *, n_valid, tn):
    a = a_ref[...]
    g = jnp.dot(a, wg_ref[...], preferred_element_type=F32)
    u = jnp.dot(a, wu_ref[...], preferred_element_type=F32)
    col = pl.program_id(1) * tn + lax.broadcasted_iota(jnp.int32, g.shape, 1)
    o_ref[...] = jnp.where(col < n_valid, _silu(g) * u, 0.0).astype(o_ref.dtype)


def ffn_up(a, wg, wu, *, tm, tn):
    m, kdim = a.shape
    _, n = wg.shape
    nb = pl.cdiv(n, tn)
    return pl.pallas_call(
        functools.partial(_ffn_up_kernel, n_valid=n, tn=tn),
        grid=(m // tm, nb),
        in_specs=[pl.BlockSpec((tm, kdim), lambda i, j: (i, 0)),
                  pl.BlockSpec((kdim, tn), lambda i, j: (0, j)),
                  pl.BlockSpec((kdim, tn), lambda i, j: (0, j))],
        out_specs=pl.BlockSpec((tm, tn), lambda i, j: (i, j)),
        out_shape=jax.ShapeDtypeStruct((m, nb * tn), BF16),
        compiler_params=_params("parallel", "parallel"),
    )(a, wg, wu)


def _pool_kernel(halo_ref, x_ref, w_ref, scale_ref, o_ref, *, tb, seq, group):
    start = (pl.program_id(0) * tb) % seq
    keep_halo = (start > 0).astype(F32)
    pos = start + 1 + lax.broadcasted_iota(jnp.int32, (tb, 1), 0)
    for gi, win in enumerate(POOL_WINDOWS):
        cols = slice(gi * group, (gi + 1) * group)
        x = x_ref[:, cols]
        s = jnp.concatenate([halo_ref[:, cols] * keep_halo, x], axis=0)
        shift = 1
        while shift < win:
            s = s + pltpu.roll(s, shift, axis=0)
            shift *= 2
        cnt = jnp.minimum(pos, win).astype(F32)
        y = s[POOL_HALO:, :] / cnt - x
        r = jnp.dot(y.astype(BF16), w_ref[gi], preferred_element_type=F32)
        o_ref[:, cols] = (r * scale_ref[:, cols]).astype(o_ref.dtype)


def pool_mixer(z, w_grp, scale, *, seq, tb=256):
    t = z.shape[0]
    n_g, group, _ = w_grp.shape
    width = n_g * group
    hb = tb // POOL_HALO
    return pl.pallas_call(
        functools.partial(_pool_kernel, tb=tb, seq=seq, group=group),
        grid=(t // tb,),
        in_specs=[pl.BlockSpec((POOL_HALO, width), lambda i: (jnp.maximum(i * hb - 1, 0), 0)),
                  pl.BlockSpec((tb, width), lambda i: (i, 0)),
                  pl.BlockSpec((n_g, group, group), lambda i: (0, 0, 0)),
                  pl.BlockSpec((1, width), lambda i: (0, 0))],
        out_specs=pl.BlockSpec((tb, width), lambda i: (i, 0)),
        out_shape=jax.ShapeDtypeStruct((t, width), BF16),
        compiler_params=_params("parallel"),
    )(z, z, w_grp, scale.reshape(1, width))


def _gdn_kernel(q_ref, k_ref, v_ref, gate_ref, gates_ref, cq_ref, ck_ref, cv_ref, nw_ref,
                o_ref, halo_ref, state_ref, *, hps, n_heads):
    c = GDN_BLOCK
    d = GDN_HEAD_DIM
    halo = SUBLANES
    heads = range(hps)
    t = pl.program_id(2)
    h0 = pl.program_id(1) * hps

    @pl.when(t == 0)
    def _():
        halo_ref[...] = jnp.zeros_like(halo_ref)
        state_ref[...] = jnp.zeros_like(state_ref)

    def conv_silu(x_ref, idx, cw_ref):
        cur = x_ref[...]
        ext = jnp.concatenate([halo_ref[idx], cur], axis=0)
        w = cw_ref[...]
        acc = cur * w[CONV_WIDTH - 1:CONV_WIDTH, :]
        for back in range(1, CONV_WIDTH):
            tap = pltpu.roll(ext, back, axis=0)[halo:, :]
            acc = acc + tap * w[CONV_WIDTH - 1 - back:CONV_WIDTH - back, :]
        halo_ref[idx] = cur[c - halo:, :]
        return _silu(acc)

    qs = conv_silu(q_ref, 0, cq_ref)
    ks = conv_silu(k_ref, 1, ck_ref)
    vs = conv_silu(v_ref, 2, cv_ref)

    ii = lax.broadcasted_iota(jnp.int32, (c, c), 0)
    jj = lax.broadcasted_iota(jnp.int32, (c, c), 1)
    causal = ii >= jj
    strict = ii > jj

    def dot(a, b):
        return jnp.dot(a.astype(BF16), b.astype(BF16), preferred_element_type=F32)

    def dot_nt(a, b):
        return lax.dot_general(a.astype(BF16), b.astype(BF16), (((1,), (1,)), ((), ())),
                               preferred_element_type=F32)

    def l2norm(x):
        return x * lax.rsqrt(jnp.sum(x * x, axis=-1, keepdims=True) + EPS)

    q = [l2norm(qs[:, j * d:(j + 1) * d]) * (d ** -0.5) for j in heads]
    k = [l2norm(ks[:, j * d:(j + 1) * d]) for j in heads]
    v = [vs[:, j * d:(j + 1) * d] for j in heads]

    beta_rows = gates_ref[pl.ds(h0, hps), :]
    gc_rows = gates_ref[pl.ds(n_heads + h0, hps), :]
    beta = [jnp.broadcast_to(beta_rows[j:j + 1, :], (c, c)).T for j in heads]
    gc_t = [jnp.broadcast_to(gc_rows[j:j + 1, :], (c, c)) for j in heads]
    gc = [g.T for g in gc_t]
    decay = [jnp.exp(jnp.where(causal, gc[j] - gc_t[j], -jnp.inf)) for j in heads]
    gamma = [jnp.exp(g) for g in gc]
    gc_last = [g[c - 1:c, :] for g in gc]
    kb = [k[j] * beta[j] for j in heads]
    lmat = [jnp.where(strict, dot_nt(kb[j], k[j]) * decay[j], 0.0) for j in heads]
    attn = [dot_nt(q[j], k[j]) * decay[j] for j in heads]

    n = [-m for m in lmat]
    p = lmat
    width = 2
    while width < c:
        p = [dot(m, m) for m in p]
        n = [n[j] + p[j] + dot(n[j], p[j]) for j in heads]
        width *= 2
    rhs = [jnp.concatenate([kb[j] * gamma[j], v[j] * beta[j]], axis=1) for j in heads]
    sol = [rhs[j] + dot(n[j], rhs[j]) for j in heads]
    qd = [q[j] * gamma[j] for j in heads]
    kd_t = [(k[j] * jnp.exp(gc_last[j] - gc[j])).T for j in heads]

    state = [state_ref[j] for j in heads]
    v_new = [sol[j][:, d:] - dot(sol[j][:, :d], state[j]) for j in heads]
    o = [dot(qd[j], state[j]) + dot(attn[j], v_new[j]) for j in heads]
    for j in heads:
        state_ref[j] = jnp.exp(gc_last[j]) * state[j] + dot(kd_t[j], v_new[j])
    for j in heads:
        gate = gate_ref[:, j * d:(j + 1) * d]
        o_ref[:, j * d:(j + 1) * d] = (_rms(o[j], nw_ref[...]) * _silu(gate)).astype(o_ref.dtype)


def gated_deltanet(z, gates, conv_w, norm_w, *, batch, seq, col0, width):
    d = GDN_HEAD_DIM
    c = GDN_BLOCK
    hps = GDN_HEADS_PER_STEP
    n_heads = width // d
    bw = hps * d
    nblk = seq // c
    cb0 = col0 // bw
    wb = width // bw

    def zspec(part):
        return pl.BlockSpec((c, bw), lambda b, h, t: (b * nblk + t, cb0 + part * wb + h))

    def cspec(part):
        return pl.BlockSpec((CONV_WIDTH, bw), lambda b, h, t: (0, part * wb + h))

    return pl.pallas_call(
        functools.partial(_gdn_kernel, hps=hps, n_heads=n_heads),
        grid=(batch, n_heads // hps, nblk),
        in_specs=[zspec(0), zspec(1), zspec(2), zspec(3),
                  pl.BlockSpec((2 * n_heads, c), lambda b, h, t: (0, b * nblk + t)),
                  cspec(0), cspec(1), cspec(2),
                  pl.BlockSpec((1, d), lambda b, h, t: (0, 0))],
        out_specs=pl.BlockSpec((c, bw), lambda b, h, t: (b * nblk + t, h)),
        out_shape=jax.ShapeDtypeStruct((batch * seq, width), BF16),
        scratch_shapes=[pltpu.VMEM((3, SUBLANES, bw), F32), pltpu.VMEM((hps, d, d), F32)],
        compiler_params=_params("parallel", "parallel", "arbitrary"),
    )(z, z, z, z, gates, conv_w, conv_w, conv_w, norm_w.reshape(1, d))


def _sgu_kernel(u_ref, v_ref, lng_ref, lnb_ref, ws_ref, bs_ref, o_ref, *, group):
    v = v_ref[...]
    mu = jnp.mean(v, axis=-1, keepdims=True)
    xc = v - mu
    vn = xc * lax.rsqrt(jnp.mean(xc * xc, axis=-1, keepdims=True) + EPS) * lng_ref[...] + lnb_ref[...]
    vb = vn.astype(BF16)
    ii = lax.broadcasted_iota(jnp.int32, (SGU_LEN, SGU_LEN), 0)
    jj = lax.broadcasted_iota(jnp.int32, (SGU_LEN, SGU_LEN), 1)
    chunk_bits = SGU_CHUNK.bit_length() - 1
    mask = (jj >> chunk_bits) <= (ii >> chunk_bits)
    for g in range(SGU_GROUPS):
        cols = slice(g * group, (g + 1) * group)
        ws = jnp.where(mask, ws_ref[g], 0.0).astype(BF16)
        sv = jnp.dot(ws, vb[:, cols], preferred_element_type=F32) + bs_ref[:, g:g + 1]
        o_ref[:, cols] = (u_ref[:, cols] * sv).astype(o_ref.dtype)


def spatial_gating(z, ln_g, ln_b, w_s, b_s):
    t, w2 = z.shape
    width = w2 // 2
    group = width // SGU_GROUPS
    vec = pl.BlockSpec((1, width), lambda i: (0, 0))
    return pl.pallas_call(
        functools.partial(_sgu_kernel, group=group),
        grid=(t // SGU_LEN,),
        in_specs=[pl.BlockSpec((SGU_LEN, width), lambda i: (i, 0)),
                  pl.BlockSpec((SGU_LEN, width), lambda i: (i, 1)),
                  vec, vec,
                  pl.BlockSpec((SGU_GROUPS, SGU_LEN, SGU_LEN), lambda i: (0, 0, 0)),
                  pl.BlockSpec((SGU_LEN, SGU_GROUPS), lambda i: (0, 0))],
        out_specs=pl.BlockSpec((SGU_LEN, width), lambda i: (i, 0)),
        out_shape=jax.ShapeDtypeStruct((t, width), BF16),
        compiler_params=_params("parallel"),
    )(z, z, ln_g.reshape(1, width), ln_b.reshape(1, width), w_s, b_s.T)


def _mix_pool_gdn(h, w_in, pool_w, pool_scale, conv_w, a_log, dt_bias, norm_w, w_out, *, batch, seq):
    n_g, group, _ = pool_w.shape
    pool_width = n_g * group
    n_heads = a_log.shape[0]
    gdn_width = n_heads * GDN_HEAD_DIM
    main = pool_width + 4 * gdn_width
    z = matmul(h, w_in[:, :main].astype(BF16), tm=1024, tn=1024)
    gates = gdn_gates(w_in[:, main:].T.astype(BF16), h, a_log, dt_bias, tn=1024)
    ya = pool_mixer(z, pool_w.astype(BF16), pool_scale, seq=seq)
    yb = gated_deltanet(z, gates, conv_w, norm_w,
                        batch=batch, seq=seq, col0=pool_width, width=gdn_width)
    return matmul(jnp.concatenate([ya, yb], axis=-1), w_out.astype(BF16), tm=1024, tn=1024)


def _mix_sgu(h, w_in, ln_g, ln_b, w_s, b_s, w_out):
    z = matmul(h, w_in.astype(BF16), tm=1024, tn=1024, act="gelu")
    s = spatial_gating(z, ln_g, ln_b, w_s, b_s)
    return matmul(s, w_out.astype(BF16), tm=1024, tn=1024)


def _swiglu(h, w_gate, w_up, w_down, *, tn=512, tk=2816):
    act = ffn_up(h, w_gate.astype(BF16), w_up.astype(BF16), tm=1024, tn=tn)
    pad = act.shape[1] - w_down.shape[0]
    wd = jnp.pad(w_down.astype(BF16), ((0, pad), (0, 0)))
    return matmul(act, wd, tm=1024, tn=1024, tk=tk)


def kernel(x, norm_mix_pre, norm_mix_post, norm_ffn_pre, norm_ffn_post, ab_w_in, pool_w, pool_scale, gdn_conv, gdn_a_log, gdn_dt_bias, gdn_norm, ab_w_out, sgu_w_in, sgu_ln_g, sgu_ln_b, sgu_w_s, sgu_b_s, sgu_w_out, ffn_w_gate, ffn_w_up, ffn_w_down):
    batch, seq, d_model = x.shape
    depth = norm_mix_pre.shape[0]
    xt = x.reshape(batch * seq, d_model)
    h = prenorm(xt, norm_mix_pre[0])
    for layer in range(depth):
        i = layer // 2
        if layer % 2 == 0:
            y = _mix_pool_gdn(h, ab_w_in[i], pool_w[i], pool_scale[i], gdn_conv[i], gdn_a_log[i],
                              gdn_dt_bias[i], gdn_norm[i], ab_w_out[i], batch=batch, seq=seq)
        else:
            y = _mix_sgu(h, sgu_w_in[i], sgu_ln_g[i], sgu_ln_b[i], sgu_w_s[i], sgu_b_s[i],
                         sgu_w_out[i])
        xt, h = postnorm_residual(y, xt, norm_mix_post[layer], norm_ffn_pre[layer])
        y = _swiglu(h, ffn_w_gate[layer], ffn_w_up[layer], ffn_w_down[layer])
        nxt = norm_mix_pre[layer + 1] if layer + 1 < depth else None
        xt, h = postnorm_residual(y, xt, norm_ffn_post[layer], nxt)
    return xt.reshape(batch, seq, d_model)
```

```python
import functools

import jax
import jax.numpy as jnp
from jax import lax
from jax.experimental import pallas as pl
from jax.experimental.pallas import tpu as pltpu

F32 = jnp.float32
BF16 = jnp.bfloat16

EPS = 1e-6
POOL_WINDOWS = (2, 4, 8, 16)
POOL_HALO = 16
GDN_HEAD_DIM = 128
CONV_WIDTH = 4
GDN_BLOCK = 128
GDN_HEADS_PER_STEP = 16
SGU_GROUPS = 16
SGU_LEN = 128
SGU_CHUNK = 64
LANES = 128
SUBLANES = 8
VMEM_LIMIT_BYTES = 56 * 1024 * 1024


def _params(*semantics):
    return pltpu.CompilerParams(dimension_semantics=semantics,
                                vmem_limit_bytes=VMEM_LIMIT_BYTES)


def _sigmoid(x):
    return 1.0 / (1.0 + jnp.exp(-x))


def _silu(x):
    return x * _sigmoid(x)


def _gelu_tanh(x):
    c = 0.7978845608028654
    return 0.5 * x * (1.0 + jnp.tanh(c * (x + 0.044715 * (x * x * x))))


def _rms(x, g):
    return x * lax.rsqrt(jnp.mean(x * x, axis=-1, keepdims=True) + EPS) * g


def _prenorm_kernel(x_ref, g_ref, h_ref):
    h_ref[...] = _rms(x_ref[...], g_ref[...]).astype(h_ref.dtype)


def prenorm(x, g, *, tr=256):
    t, d = x.shape
    return pl.pallas_call(
        _prenorm_kernel,
        grid=(t // tr,),
        in_specs=[pl.BlockSpec((tr, d), lambda i: (i, 0)),
                  pl.BlockSpec((1, d), lambda i: (0, 0))],
        out_specs=pl.BlockSpec((tr, d), lambda i: (i, 0)),
        out_shape=jax.ShapeDtypeStruct((t, d), BF16),
        compiler_params=_params("parallel"),
    )(x, g.reshape(1, d))


def _postnorm_kernel(y_ref, x_ref, gpost_ref, *rest, with_next):
    xn = x_ref[...] + _rms(y_ref[...], gpost_ref[...])
    if with_next:
        gpre_ref, xo_ref, h_ref = rest
        h_ref[...] = _rms(xn, gpre_ref[...]).astype(h_ref.dtype)
    else:
        (xo_ref,) = rest
    xo_ref[...] = xn


def postnorm_residual(y, x, g_post, g_pre_next=None, *, tr=256):
    t, d = x.shape
    with_next = g_pre_next is not None
    row = pl.BlockSpec((tr, d), lambda i: (i, 0))
    vec = pl.BlockSpec((1, d), lambda i: (0, 0))
    in_specs = [row, row, vec]
    args = [y, x, g_post.reshape(1, d)]
    out_specs = [row]
    out_shape = [jax.ShapeDtypeStruct((t, d), F32)]
    if with_next:
        in_specs.append(vec)
        args.append(g_pre_next.reshape(1, d))
        out_specs.append(row)
        out_shape.append(jax.ShapeDtypeStruct((t, d), BF16))
    out = pl.pallas_call(
        functools.partial(_postnorm_kernel, with_next=with_next),
        grid=(t // tr,),
        in_specs=in_specs,
        out_specs=out_specs,
        out_shape=out_shape,
        compiler_params=_params("parallel"),
    )(*args)
    return (out[0], out[1]) if with_next else (out[0], None)


def _mm_wres_kernel(a_ref, w_ref, o_ref, wb_ref, *, act):
    @pl.when(pl.program_id(1) == 0)
    def _():
        wb_ref[...] = w_ref[...].astype(BF16)

    r = jnp.dot(a_ref[...], wb_ref[...], preferred_element_type=F32)
    if act == "gelu":
        r = _gelu_tanh(r)
    o_ref[...] = r.astype(o_ref.dtype)


def matmul_wres(a, w, layer, *, tm, tn, n_cols=None, out_dtype=F32, act=None):
    m, kdim = a.shape
    n = w.shape[2] if n_cols is None else n_cols
    assert m % tm == 0 and n % tn == 0
    return pl.pallas_call(
        functools.partial(_mm_wres_kernel, act=act),
        grid=(n // tn, m // tm),
        in_specs=[pl.BlockSpec((tm, kdim), lambda j, i: (i, 0)),
                  pl.BlockSpec((None, kdim, tn), lambda j, i: (layer, 0, j))],
        out_specs=pl.BlockSpec((tm, tn), lambda j, i: (i, j)),
        out_shape=jax.ShapeDtypeStruct((m, n), out_dtype),
        scratch_shapes=[pltpu.VMEM((kdim, tn), BF16)],
        compiler_params=_params("arbitrary", "arbitrary"),
    )(a, w)


def _mm_kernel(a_ref, w_ref, o_ref):
    o_ref[...] = jnp.dot(a_ref[...], w_ref[...], preferred_element_type=F32)


def matmul(a, w, *, tm, tn):
    m, kdim = a.shape
    _, n = w.shape
    assert m % tm == 0 and n % tn == 0
    return pl.pallas_call(
        _mm_kernel,
        grid=(m // tm, n // tn),
        in_specs=[pl.BlockSpec((tm, kdim), lambda i, j: (i, 0)),
                  pl.BlockSpec((kdim, tn), lambda i, j: (0, j))],
        out_specs=pl.BlockSpec((tm, tn), lambda i, j: (i, j)),
        out_shape=jax.ShapeDtypeStruct((m, n), F32),
        compiler_params=_params("parallel", "parallel"),
    )(a, w)


def _gdn_gates_kernel(w_ref, a_ref, alog_ref, dtb_ref, o_ref, *, n_heads):
    logits = lax.dot_general(w_ref[...], a_ref[...], (((1,), (1,)), ((), ())),
                             preferred_element_type=F32)
    o_ref[:n_heads, :] = _sigmoid(logits[:n_heads, :])
    dl = logits[n_heads:, :] + dtb_ref[...]
    softplus = jnp.maximum(dl, 0.0) + jnp.log1p(jnp.exp(-jnp.abs(dl)))
    s = -jnp.exp(alog_ref[...]) * softplus
    pos = lax.broadcasted_iota(jnp.int32, s.shape, 1) & (GDN_BLOCK - 1)
    shift = 1
    while shift < GDN_BLOCK:
        s = s + jnp.where(pos >= shift, pltpu.roll(s, shift, axis=1), 0.0)
        shift *= 2
    o_ref[n_heads:, :] = s


def gdn_gates(wt, a, a_log, dt_bias, *, tn):
    r, kdim = wt.shape
    n_heads = r // 2
    m, _ = a.shape
    col = pl.BlockSpec((n_heads, 1), lambda i: (0, 0))
    return pl.pallas_call(
        functools.partial(_gdn_gates_kernel, n_heads=n_heads),
        grid=(m // tn,),
        in_specs=[pl.BlockSpec((r, kdim), lambda i: (0, 0)),
                  pl.BlockSpec((tn, kdim), lambda i: (i, 0)), col, col],
        out_specs=pl.BlockSpec((r, tn), lambda i: (0, i)),
        out_shape=jax.ShapeDtypeStruct((r, m), F32),
        compiler_params=_params("parallel"),
    )(wt, a, a_log.astype(F32).reshape(n_heads, 1), dt_bias.astype(F32).reshape(n_heads, 1))


def _ffn_up_kernel(a_ref, wg_ref, wu_ref, wd_ref, o_ref, wd_bf_ref, wb_ref):
    @pl.when(pl.program_id(1) == 0)
    def _():
        wb_ref[0] = wg_ref[...].astype(BF16)
        wb_ref[1] = wu_ref[...].astype(BF16)
        wd_bf_ref[...] = wd_ref[...].astype(BF16)

    a = a_ref[...]
    g = jnp.dot(a, wb_ref[0], preferred_element_type=F32)
    u = jnp.dot(a, wb_ref[1], preferred_element_type=F32)
    o_ref[...] = (_silu(g) * u).astype(o_ref.dtype)


def ffn_up(a, wg, wu, wd, layer, *, tm, tn):
    m, kdim = a.shape
    n = wg.shape[2]
    d_out = wd.shape[2]
    assert m % tm == 0 and n % tn == 0
    wspec = pl.BlockSpec((None, kdim, tn), lambda j, i: (layer, 0, j))
    return pl.pallas_call(
        _ffn_up_kernel,
        grid=(n // tn, m // tm),
        in_specs=[pl.BlockSpec((tm, kdim), lambda j, i: (i, 0)), wspec, wspec,
                  pl.BlockSpec((None, tn, d_out), lambda j, i: (layer, j, 0))],
        out_specs=[pl.BlockSpec((tm, tn), lambda j, i: (i, j)),
                   pl.BlockSpec((tn, d_out), lambda j, i: (j, 0))],
        out_shape=[jax.ShapeDtypeStruct((m, n), BF16), jax.ShapeDtypeStruct((n, d_out), BF16)],
        scratch_shapes=[pltpu.VMEM((2, kdim, tn), BF16)],
        compiler_params=_params("arbitrary", "arbitrary"),
    )(a, wg, wu, wd)


def _pool_kernel(halo_ref, x_ref, w_ref, scale_ref, o_ref, *, tb, seq, group):
    start = (pl.program_id(0) * tb) % seq
    keep_halo = (start > 0).astype(F32)
    pos = start + 1 + lax.broadcasted_iota(jnp.int32, (tb, 1), 0)
    for gi, win in enumerate(POOL_WINDOWS):
        cols = slice(gi * group, (gi + 1) * group)
        x = x_ref[:, cols]
        s = jnp.concatenate([halo_ref[:, cols] * keep_halo, x], axis=0)
        shift = 1
        while shift < win:
            s = s + pltpu.roll(s, shift, axis=0)
            shift *= 2
        cnt = jnp.minimum(pos, win).astype(F32)
        y = s[POOL_HALO:, :] / cnt - x
        r = jnp.dot(y.astype(BF16), w_ref[gi], preferred_element_type=F32)
        o_ref[:, cols] = (r * scale_ref[:, cols]).astype(o_ref.dtype)


def pool_mixer(z, w_grp, scale, *, seq, out_width, tb=256):
    t = z.shape[0]
    n_g, group, _ = w_grp.shape
    width = n_g * group
    hb = tb // POOL_HALO
    return pl.pallas_call(
        functools.partial(_pool_kernel, tb=tb, seq=seq, group=group),
        grid=(t // tb,),
        in_specs=[pl.BlockSpec((POOL_HALO, width), lambda i: (jnp.maximum(i * hb - 1, 0), 0)),
                  pl.BlockSpec((tb, width), lambda i: (i, 0)),
                  pl.BlockSpec((n_g, group, group), lambda i: (0, 0, 0)),
                  pl.BlockSpec((1, width), lambda i: (0, 0))],
        out_specs=pl.BlockSpec((tb, width), lambda i: (i, 0)),
        out_shape=jax.ShapeDtypeStruct((t, out_width), BF16),
        compiler_params=_params("parallel"),
    )(z, z, w_grp, scale.reshape(1, width))


def _gdn_kernel(q_ref, k_ref, v_ref, gate_ref, gates_ref, cq_ref, ck_ref, cv_ref, nw_ref,
                y_hbm_ref, o_ref, halo_ref, state_ref, *, hps, n_heads):
    del y_hbm_ref
    c = GDN_BLOCK
    d = GDN_HEAD_DIM
    halo = SUBLANES
    heads = range(hps)
    t = pl.program_id(2)
    h0 = pl.program_id(1) * hps

    @pl.when(t == 0)
    def _():
        halo_ref[...] = jnp.zeros_like(halo_ref)
        state_ref[...] = jnp.zeros_like(state_ref)

    def conv_silu(x_ref, idx, cw_ref):
        cur = x_ref[...]
        ext = jnp.concatenate([halo_ref[idx], cur], axis=0)
        w = cw_ref[...]
        acc = cur * w[CONV_WIDTH - 1:CONV_WIDTH, :]
        for back in range(1, CONV_WIDTH):
            tap = pltpu.roll(ext, back, axis=0)[halo:, :]
            acc = acc + tap * w[CONV_WIDTH - 1 - back:CONV_WIDTH - back, :]
        halo_ref[idx] = cur[c - halo:, :]
        return _silu(acc)

    qs = conv_silu(q_ref, 0, cq_ref)
    ks = conv_silu(k_ref, 1, ck_ref)
    vs = conv_silu(v_ref, 2, cv_ref)

    ii = lax.broadcasted_iota(jnp.int32, (c, c), 0)
    jj = lax.broadcasted_iota(jnp.int32, (c, c), 1)
    causal = ii >= jj
    strict = ii > jj

    def dot(a, b):
        return jnp.dot(a.astype(BF16), b.astype(BF16), preferred_element_type=F32)

    def dot_nt(a, b):
        return lax.dot_general(a.astype(BF16), b.astype(BF16), (((1,), (1,)), ((), ())),
                               preferred_element_type=F32)

    def l2norm(x):
        return x * lax.rsqrt(jnp.sum(x * x, axis=-1, keepdims=True) + EPS)

    q = [l2norm(qs[:, j * d:(j + 1) * d]) * (d ** -0.5) for j in heads]
    k = [l2norm(ks[:, j * d:(j + 1) * d]) for j in heads]
    v = [vs[:, j * d:(j + 1) * d] for j in heads]

    beta_rows = gates_ref[pl.ds(h0, hps), :]
    gc_rows = gates_ref[pl.ds(n_heads + h0, hps), :]
    beta = [jnp.broadcast_to(beta_rows[j:j + 1, :], (c, c)).T for j in heads]
    gc_t = [jnp.broadcast_to(gc_rows[j:j + 1, :], (c, c)) for j in heads]
    gc = [g.T for g in gc_t]
    decay = [jnp.exp(jnp.where(causal, gc[j] - gc_t[j], -jnp.inf)) for j in heads]
    gamma = [jnp.exp(g) for g in gc]
    gc_last = [g[c - 1:c, :] for g in gc]
    kb = [k[j] * beta[j] for j in heads]
    lmat = [jnp.where(strict, dot_nt(kb[j], k[j]) * decay[j], 0.0) for j in heads]
    attn = [dot_nt(q[j], k[j]) * decay[j] for j in heads]

    n = [-m for m in lmat]
    p = lmat
    width = 2
    while width < c:
        p = [dot(m, m) for m in p]
        n = [n[j] + p[j] + dot(n[j], p[j]) for j in heads]
        width *= 2
    rhs = [jnp.concatenate([kb[j] * gamma[j], v[j] * beta[j]], axis=1) for j in heads]
    sol = [rhs[j] + dot(n[j], rhs[j]) for j in heads]
    qd = [q[j] * gamma[j] for j in heads]
    kd_t = [(k[j] * jnp.exp(gc_last[j] - gc[j])).T for j in heads]

    state = [state_ref[j] for j in heads]
    v_new = [sol[j][:, d:] - dot(sol[j][:, :d], state[j]) for j in heads]
    o = [dot(qd[j], state[j]) + dot(attn[j], v_new[j]) for j in heads]
    for j in heads:
        state_ref[j] = jnp.exp(gc_last[j]) * state[j] + dot(kd_t[j], v_new[j])
    for j in heads:
        gate = gate_ref[:, j * d:(j + 1) * d]
        o_ref[:, j * d:(j + 1) * d] = (_rms(o[j], nw_ref[...]) * _silu(gate)).astype(o_ref.dtype)


def gated_deltanet(z, gates, conv_w, norm_w, y_mix, *, batch, seq, col0, width):
    d = GDN_HEAD_DIM
    c = GDN_BLOCK
    hps = GDN_HEADS_PER_STEP
    n_heads = width // d
    bw = hps * d
    nblk = seq // c
    cb0 = col0 // bw
    wb = width // bw
    ocb0 = (y_mix.shape[1] - width) // bw

    def zspec(part):
        return pl.BlockSpec((c, bw), lambda b, h, t: (b * nblk + t, cb0 + part * wb + h))

    def cspec(part):
        return pl.BlockSpec((CONV_WIDTH, bw), lambda b, h, t: (0, part * wb + h))

    return pl.pallas_call(
        functools.partial(_gdn_kernel, hps=hps, n_heads=n_heads),
        grid=(batch, n_heads // hps, nblk),
        in_specs=[zspec(0), zspec(1), zspec(2), zspec(3),
                  pl.BlockSpec((2 * n_heads, c), lambda b, h, t: (0, b * nblk + t)),
                  cspec(0), cspec(1), cspec(2),
                  pl.BlockSpec((1, d), lambda b, h, t: (0, 0)),
                  pl.BlockSpec(memory_space=pl.ANY)],
        out_specs=pl.BlockSpec((c, bw), lambda b, h, t: (b * nblk + t, ocb0 + h)),
        out_shape=jax.ShapeDtypeStruct(y_mix.shape, y_mix.dtype),
        input_output_aliases={9: 0},
        scratch_shapes=[pltpu.VMEM((3, SUBLANES, bw), F32), pltpu.VMEM((hps, d, d), F32)],
        compiler_params=_params("parallel", "parallel", "arbitrary"),
    )(z, z, z, z, gates, conv_w, conv_w, conv_w, norm_w.reshape(1, d), y_mix)


def _sgu_kernel(u_ref, v_ref, lng_ref, lnb_ref, ws_ref, bs_ref, o_ref, *, group):
    v = v_ref[...]
    mu = jnp.mean(v, axis=-1, keepdims=True)
    xc = v - mu
    vn = xc * lax.rsqrt(jnp.mean(xc * xc, axis=-1, keepdims=True) + EPS) * lng_ref[...] + lnb_ref[...]
    vb = vn.astype(BF16)
    ii = lax.broadcasted_iota(jnp.int32, (SGU_LEN, SGU_LEN), 0)
    jj = lax.broadcasted_iota(jnp.int32, (SGU_LEN, SGU_LEN), 1)
    chunk_bits = SGU_CHUNK.bit_length() - 1
    mask = (jj >> chunk_bits) <= (ii >> chunk_bits)
    for g in range(SGU_GROUPS):
        cols = slice(g * group, (g + 1) * group)
        ws = jnp.where(mask, ws_ref[g], 0.0).astype(BF16)
        sv = jnp.dot(ws, vb[:, cols], preferred_element_type=F32) + bs_ref[:, g:g + 1]
        o_ref[:, cols] = (u_ref[:, cols] * sv).astype(o_ref.dtype)


def spatial_gating(z, ln_g, ln_b, w_s, b_s):
    t, w2 = z.shape
    width = w2 // 2
    group = width // SGU_GROUPS
    vec = pl.BlockSpec((1, width), lambda i: (0, 0))
    return pl.pallas_call(
        functools.partial(_sgu_kernel, group=group),
        grid=(t // SGU_LEN,),
        in_specs=[pl.BlockSpec((SGU_LEN, width), lambda i: (i, 0)),
                  pl.BlockSpec((SGU_LEN, width), lambda i: (i, 1)),
                  vec, vec,
                  pl.BlockSpec((SGU_GROUPS, SGU_LEN, SGU_LEN), lambda i: (0, 0, 0)),
                  pl.BlockSpec((SGU_LEN, SGU_GROUPS), lambda i: (0, 0))],
        out_specs=pl.BlockSpec((SGU_LEN, width), lambda i: (i, 0)),
        out_shape=jax.ShapeDtypeStruct((t, width), BF16),
        compiler_params=_params("parallel"),
    )(z, z, ln_g.reshape(1, width), ln_b.reshape(1, width), w_s, b_s.T)


def _mix_pool_gdn(h, i, w_in, pool_w, pool_scale, conv_w, a_log, dt_bias, norm_w, w_out, *, batch, seq):
    n_g, group, _ = pool_w.shape
    pool_width = n_g * group
    n_heads = a_log.shape[0]
    gdn_width = n_heads * GDN_HEAD_DIM
    main = pool_width + 4 * gdn_width
    z = matmul_wres(h, w_in, i, n_cols=main, tm=1024, tn=512)
    gates = gdn_gates(w_in[i, :, main:].T.astype(BF16), h, a_log, dt_bias, tn=1024)
    y = pool_mixer(z, pool_w.astype(BF16), pool_scale, seq=seq, out_width=pool_width + gdn_width)
    y = gated_deltanet(z, gates, conv_w, norm_w, y, batch=batch, seq=seq, col0=pool_width,
                       width=gdn_width)
    return matmul_wres(y, w_out, i, tm=1024, tn=512)


def _mix_sgu(h, i, w_in, ln_g, ln_b, w_s, b_s, w_out):
    z = matmul_wres(h, w_in, i, tm=1024, tn=512, act="gelu")
    s = spatial_gating(z, ln_g, ln_b, w_s, b_s)
    return matmul_wres(s, w_out, i, tm=1024, tn=512)


def _swiglu(h, layer, w_gate, w_up, w_down):
    act, wd = ffn_up(h, w_gate, w_up, w_down, layer, tm=1024, tn=256)
    return matmul(act, wd, tm=512, tn=512)


def kernel(x, norm_mix_pre, norm_mix_post, norm_ffn_pre, norm_ffn_post, ab_w_in, pool_w, pool_scale, gdn_conv, gdn_a_log, gdn_dt_bias, gdn_norm, ab_w_out, sgu_w_in, sgu_ln_g, sgu_ln_b, sgu_w_s, sgu_b_s, sgu_w_out, ffn_w_gate, ffn_w_up, ffn_w_down):
    batch, seq, d_model = x.shape
    depth = norm_mix_pre.shape[0]
    xt = x.reshape(batch * seq, d_model)
    h = prenorm(xt, norm_mix_pre[0])
    for layer in range(depth):
        i = layer // 2
        if layer % 2 == 0:
            y = _mix_pool_gdn(h, i, ab_w_in, pool_w[i], pool_scale[i], gdn_conv[i], gdn_a_log[i],
                              gdn_dt_bias[i], gdn_norm[i], ab_w_out, batch=batch, seq=seq)
        else:
            y = _mix_sgu(h, i, sgu_w_in, sgu_ln_g[i], sgu_ln_b[i], sgu_w_s[i], sgu_b_s[i], sgu_w_out)
        xt, h = postnorm_residual(y, xt, norm_mix_post[layer], norm_ffn_pre[layer])
        y = _swiglu(h, layer, ffn_w_gate, ffn_w_up, ffn_w_down)
        nxt = norm_mix_pre[layer + 1] if layer + 1 < depth else None
        xt, h = postnorm_residual(y, xt, norm_ffn_post[layer], nxt)
    return xt.reshape(batch, seq, d_model)
```

```python
import functools

import jax
import jax.numpy as jnp
from jax import lax
from jax.experimental import pallas as pl
from jax.experimental.pallas import tpu as pltpu

F32 = jnp.float32
BF16 = jnp.bfloat16

EPS = 1e-6
POOL_WINDOWS = (2, 4, 8, 16)
POOL_HALO = 16
GDN_HEAD_DIM = 128
CONV_WIDTH = 4
GDN_BLOCK = 128
GDN_HEADS_PER_STEP = 16
SGU_GROUPS = 16
SGU_LEN = 128
SGU_CHUNK = 64
LANES = 128
SUBLANES = 8
VMEM_LIMIT_BYTES = 56 * 1024 * 1024


def _params(*semantics):
    return pltpu.CompilerParams(dimension_semantics=semantics,
                                vmem_limit_bytes=VMEM_LIMIT_BYTES)


def _sigmoid(x):
    return 1.0 / (1.0 + jnp.exp(-x))


def _silu(x):
    return x * _sigmoid(x)


def _gelu_tanh(x):
    c = 0.7978845608028654
    return 0.5 * x * (1.0 + jnp.tanh(c * (x + 0.044715 * (x * x * x))))


def _rms(x, g):
    return x * lax.rsqrt(jnp.mean(x * x, axis=-1, keepdims=True) + EPS) * g


def _prenorm_kernel(x_ref, g_ref, h_ref):
    h_ref[...] = _rms(x_ref[...], g_ref[...]).astype(h_ref.dtype)


def prenorm(x, g, *, tr=256):
    t, d = x.shape
    return pl.pallas_call(
        _prenorm_kernel,
        grid=(t // tr,),
        in_specs=[pl.BlockSpec((tr, d), lambda i: (i, 0)),
                  pl.BlockSpec((1, d), lambda i: (0, 0))],
        out_specs=pl.BlockSpec((tr, d), lambda i: (i, 0)),
        out_shape=jax.ShapeDtypeStruct((t, d), BF16),
        compiler_params=_params("parallel"),
    )(x, g.reshape(1, d))


def _postnorm_kernel(y_ref, x_ref, gpost_ref, *rest, with_next):
    xn = x_ref[...] + _rms(y_ref[...], gpost_ref[...])
    if with_next:
        gpre_ref, xo_ref, h_ref = rest
        h_ref[...] = _rms(xn, gpre_ref[...]).astype(h_ref.dtype)
    else:
        (xo_ref,) = rest
    xo_ref[...] = xn


def postnorm_residual(y, x, g_post, g_pre_next=None, *, tr=256):
    t, d = x.shape
    with_next = g_pre_next is not None
    row = pl.BlockSpec((tr, d), lambda i: (i, 0))
    vec = pl.BlockSpec((1, d), lambda i: (0, 0))
    in_specs = [row, row, vec]
    args = [y, x, g_post.reshape(1, d)]
    out_specs = [row]
    out_shape = [jax.ShapeDtypeStruct((t, d), F32)]
    if with_next:
        in_specs.append(vec)
        args.append(g_pre_next.reshape(1, d))
        out_specs.append(row)
        out_shape.append(jax.ShapeDtypeStruct((t, d), BF16))
    out = pl.pallas_call(
        functools.partial(_postnorm_kernel, with_next=with_next),
        grid=(t // tr,),
        in_specs=in_specs,
        out_specs=out_specs,
        out_shape=out_shape,
        compiler_params=_params("parallel"),
    )(*args)
    return (out[0], out[1]) if with_next else (out[0], None)


def _mm_stream_kernel(a_ref, w_ref, o_ref, *, act):
    r = jnp.dot(a_ref[...], w_ref[...].astype(BF16), preferred_element_type=F32)
    if act == "gelu":
        r = _gelu_tanh(r)
    o_ref[...] = r.astype(o_ref.dtype)


def matmul_wstream(a, w, layer, *, tm, tn, n_cols=None, out_dtype=F32, act=None):
    m, kdim = a.shape
    n = w.shape[2] if n_cols is None else n_cols
    assert m % tm == 0 and n % tn == 0
    return pl.pallas_call(
        functools.partial(_mm_stream_kernel, act=act),
        grid=(m // tm, n // tn),
        in_specs=[pl.BlockSpec((tm, kdim), lambda i, j: (i, 0)),
                  pl.BlockSpec((None, kdim, tn), lambda i, j: (layer, 0, j))],
        out_specs=pl.BlockSpec((tm, tn), lambda i, j: (i, j)),
        out_shape=jax.ShapeDtypeStruct((m, n), out_dtype),
        compiler_params=_params("parallel", "parallel"),
    )(a, w)


def _mm_kernel(a_ref, w_ref, o_ref):
    o_ref[...] = jnp.dot(a_ref[...], w_ref[...], preferred_element_type=F32)


def matmul(a, w, *, tm, tn):
    m, kdim = a.shape
    _, n = w.shape
    assert m % tm == 0 and n % tn == 0
    return pl.pallas_call(
        _mm_kernel,
        grid=(m // tm, n // tn),
        in_specs=[pl.BlockSpec((tm, kdim), lambda i, j: (i, 0)),
                  pl.BlockSpec((kdim, tn), lambda i, j: (0, j))],
        out_specs=pl.BlockSpec((tm, tn), lambda i, j: (i, j)),
        out_shape=jax.ShapeDtypeStruct((m, n), F32),
        compiler_params=_params("parallel", "parallel"),
    )(a, w)


def _gdn_gates_kernel(a_ref, w_ref, alog_ref, dtb_ref, o_ref, wb_ref, *, n_heads):
    @pl.when(pl.program_id(0) == 0)
    def _():
        wb_ref[...] = w_ref[...].T.astype(BF16)

    logits = lax.dot_general(wb_ref[...], a_ref[...], (((1,), (1,)), ((), ())),
                             preferred_element_type=F32)
    o_ref[:n_heads, :] = _sigmoid(logits[:n_heads, :])
    dl = logits[n_heads:2 * n_heads, :] + dtb_ref[...]
    softplus = jnp.maximum(dl, 0.0) + jnp.log1p(jnp.exp(-jnp.abs(dl)))
    s = -jnp.exp(alog_ref[...]) * softplus
    pos = lax.broadcasted_iota(jnp.int32, s.shape, 1) & (GDN_BLOCK - 1)
    shift = 1
    while shift < GDN_BLOCK:
        s = s + jnp.where(pos >= shift, pltpu.roll(s, shift, axis=1), 0.0)
        shift *= 2
    o_ref[n_heads:, :] = s


def gdn_gates(a, w, layer, col0, a_log, dt_bias, *, tn):
    n_heads = a_log.shape[0]
    m, kdim = a.shape
    assert col0 % LANES == 0 and 2 * n_heads <= LANES
    col = pl.BlockSpec((n_heads, 1), lambda i: (0, 0))
    return pl.pallas_call(
        functools.partial(_gdn_gates_kernel, n_heads=n_heads),
        grid=(m // tn,),
        in_specs=[pl.BlockSpec((tn, kdim), lambda i: (i, 0)),
                  pl.BlockSpec((None, kdim, LANES), lambda i: (layer, 0, col0 // LANES)),
                  col, col],
        out_specs=pl.BlockSpec((2 * n_heads, tn), lambda i: (0, i)),
        out_shape=jax.ShapeDtypeStruct((2 * n_heads, m), F32),
        scratch_shapes=[pltpu.VMEM((LANES, kdim), BF16)],
        compiler_params=_params("arbitrary"),
    )(a, w, a_log.astype(F32).reshape(n_heads, 1), dt_bias.astype(F32).reshape(n_heads, 1))


def _ffn_up_kernel(a_ref, wg_ref, wu_ref, wd_ref, o_ref, wd_bf_ref):
    @pl.when(pl.program_id(0) == 0)
    def _():
        wd_bf_ref[...] = wd_ref[...].astype(BF16)

    a = a_ref[...]
    g = jnp.dot(a, wg_ref[...].astype(BF16), preferred_element_type=F32)
    u = jnp.dot(a, wu_ref[...].astype(BF16), preferred_element_type=F32)
    o_ref[...] = (_silu(g) * u).astype(o_ref.dtype)


def ffn_up(a, wg, wu, wd, layer, *, tm, tn):
    m, kdim = a.shape
    n = wg.shape[2]
    d_out = wd.shape[2]
    assert m % tm == 0 and n % tn == 0
    nb = n // tn
    wspec = pl.BlockSpec((None, kdim, tn), lambda i, j: (layer, 0, j))
    wd_rows = lambda i, j: jnp.where(i == 0, j, nb - 1)
    return pl.pallas_call(
        _ffn_up_kernel,
        grid=(m // tm, nb),
        in_specs=[pl.BlockSpec((tm, kdim), lambda i, j: (i, 0)), wspec, wspec,
                  pl.BlockSpec((None, tn, d_out), lambda i, j: (layer, wd_rows(i, j), 0))],
        out_specs=[pl.BlockSpec((tm, tn), lambda i, j: (i, j)),
                   pl.BlockSpec((tn, d_out), lambda i, j: (wd_rows(i, j), 0))],
        out_shape=[jax.ShapeDtypeStruct((m, n), BF16), jax.ShapeDtypeStruct((n, d_out), BF16)],
        compiler_params=_params("arbitrary", "arbitrary"),
    )(a, wg, wu, wd)


def _pool_kernel(halo_ref, x_ref, w_ref, scale_ref, o_ref, *, tb, seq, group):
    start = (pl.program_id(0) * tb) % seq
    keep_halo = (start > 0).astype(F32)
    pos = start + 1 + lax.broadcasted_iota(jnp.int32, (tb, 1), 0)
    for gi, win in enumerate(POOL_WINDOWS):
        cols = slice(gi * group, (gi + 1) * group)
        x = x_ref[:, cols]
        s = jnp.concatenate([halo_ref[:, cols] * keep_halo, x], axis=0)
        shift = 1
        while shift < win:
            s = s + pltpu.roll(s, shift, axis=0)
            shift *= 2
        cnt = jnp.minimum(pos, win).astype(F32)
        y = s[POOL_HALO:, :] / cnt - x
        r = jnp.dot(y.astype(BF16), w_ref[gi], preferred_element_type=F32)
        o_ref[:, cols] = (r * scale_ref[:, cols]).astype(o_ref.dtype)


def pool_mixer(z, w_grp, scale, *, seq, out_width, tb=256):
    t = z.shape[0]
    n_g, group, _ = w_grp.shape
    width = n_g * group
    hb = tb // POOL_HALO
    return pl.pallas_call(
        functools.partial(_pool_kernel, tb=tb, seq=seq, group=group),
        grid=(t // tb,),
        in_specs=[pl.BlockSpec((POOL_HALO, width), lambda i: (jnp.maximum(i * hb - 1, 0), 0)),
                  pl.BlockSpec((tb, width), lambda i: (i, 0)),
                  pl.BlockSpec((n_g, group, group), lambda i: (0, 0, 0)),
                  pl.BlockSpec((1, width), lambda i: (0, 0))],
        out_specs=pl.BlockSpec((tb, width), lambda i: (i, 0)),
        out_shape=jax.ShapeDtypeStruct((t, out_width), BF16),
        compiler_params=_params("parallel"),
    )(z, z, w_grp, scale.reshape(1, width))


def _gdn_kernel(q_ref, k_ref, v_ref, gate_ref, gates_ref, cq_ref, ck_ref, cv_ref, nw_ref,
                y_hbm_ref, o_ref, halo_ref, state_ref, *, hps, n_heads):
    del y_hbm_ref
    c = GDN_BLOCK
    d = GDN_HEAD_DIM
    halo = SUBLANES
    heads = range(hps)
    t = pl.program_id(2)
    h0 = pl.program_id(1) * hps

    @pl.when(t == 0)
    def _():
        halo_ref[...] = jnp.zeros_like(halo_ref)
        state_ref[...] = jnp.zeros_like(state_ref)

    def conv_silu(x_ref, idx, cw_ref):
        cur = x_ref[...]
        ext = jnp.concatenate([halo_ref[idx], cur], axis=0)
        w = cw_ref[...]
        acc = cur * w[CONV_WIDTH - 1:CONV_WIDTH, :]
        for back in range(1, CONV_WIDTH):
            tap = pltpu.roll(ext, back, axis=0)[halo:, :]
            acc = acc + tap * w[CONV_WIDTH - 1 - back:CONV_WIDTH - back, :]
        halo_ref[idx] = cur[c - halo:, :]
        return _silu(acc)

    qs = conv_silu(q_ref, 0, cq_ref)
    ks = conv_silu(k_ref, 1, ck_ref)
    vs = conv_silu(v_ref, 2, cv_ref)

    ii = lax.broadcasted_iota(jnp.int32, (c, c), 0)
    jj = lax.broadcasted_iota(jnp.int32, (c, c), 1)
    causal = ii >= jj
    strict = ii > jj

    def dot(a, b):
        return jnp.dot(a.astype(BF16), b.astype(BF16), preferred_element_type=F32)

    def dot_nt(a, b):
        return lax.dot_general(a.astype(BF16), b.astype(BF16), (((1,), (1,)), ((), ())),
                               preferred_element_type=F32)

    def l2norm(x):
        return x * lax.rsqrt(jnp.sum(x * x, axis=-1, keepdims=True) + EPS)

    q = [l2norm(qs[:, j * d:(j + 1) * d]) * (d ** -0.5) for j in heads]
    k = [l2norm(ks[:, j * d:(j + 1) * d]) for j in heads]
    v = [vs[:, j * d:(j + 1) * d] for j in heads]

    beta_rows = gates_ref[pl.ds(h0, hps), :]
    gc_rows = gates_ref[pl.ds(n_heads + h0, hps), :]
    beta = [jnp.broadcast_to(beta_rows[j:j + 1, :], (c, c)).T for j in heads]
    gc_t = [jnp.broadcast_to(gc_rows[j:j + 1, :], (c, c)) for j in heads]
    gc = [g.T for g in gc_t]
    decay = [jnp.exp(jnp.where(causal, gc[j] - gc_t[j], -jnp.inf)) for j in heads]
    gamma = [jnp.exp(g) for g in gc]
    gc_last = [g[c - 1:c, :] for g in gc]
    kb = [k[j] * beta[j] for j in heads]
    lmat = [jnp.where(strict, dot_nt(kb[j], k[j]) * decay[j], 0.0) for j in heads]
    attn = [dot_nt(q[j], k[j]) * decay[j] for j in heads]

    n = [-m for m in lmat]
    p = lmat
    width = 2
    while width < c:
        p = [dot(m, m) for m in p]
        n = [n[j] + p[j] + dot(n[j], p[j]) for j in heads]
        width *= 2
    rhs = [jnp.concatenate([kb[j] * gamma[j], v[j] * beta[j]], axis=1) for j in heads]
    sol = [rhs[j] + dot(n[j], rhs[j]) for j in heads]
    qd = [q[j] * gamma[j] for j in heads]
    kd_t = [(k[j] * jnp.exp(gc_last[j] - gc[j])).T for j in heads]

    state = [state_ref[j] for j in heads]
    v_new = [sol[j][:, d:] - dot(sol[j][:, :d], state[j]) for j in heads]
    o = [dot(qd[j], state[j]) + dot(attn[j], v_new[j]) for j in heads]
    for j in heads:
        state_ref[j] = jnp.exp(gc_last[j]) * state[j] + dot(kd_t[j], v_new[j])
    for j in heads:
        gate = gate_ref[:, j * d:(j + 1) * d]
        o_ref[:, j * d:(j + 1) * d] = (_rms(o[j], nw_ref[...]) * _silu(gate)).astype(o_ref.dtype)


def gated_deltanet(z, gates, conv_w, norm_w, y_mix, *, batch, seq, col0, width):
    d = GDN_HEAD_DIM
    c = GDN_BLOCK
    hps = GDN_HEADS_PER_STEP
    n_heads = width // d
    bw = hps * d
    nblk = seq // c
    cb0 = col0 // bw
    wb = width // bw
    ocb0 = (y_mix.shape[1] - width) // bw

    def zspec(part):
        return pl.BlockSpec((c, bw), lambda b, h, t: (b * nblk + t, cb0 + part * wb + h))

    def cspec(part):
        return pl.BlockSpec((CONV_WIDTH, bw), lambda b, h, t: (0, part * wb + h))

    return pl.pallas_call(
        functools.partial(_gdn_kernel, hps=hps, n_heads=n_heads),
        grid=(batch, n_heads // hps, nblk),
        in_specs=[zspec(0), zspec(1), zspec(2), zspec(3),
                  pl.BlockSpec((2 * n_heads, c), lambda b, h, t: (0, b * nblk + t)),
                  cspec(0), cspec(1), cspec(2),
                  pl.BlockSpec((1, d), lambda b, h, t: (0, 0)),
                  pl.BlockSpec(memory_space=pl.ANY)],
        out_specs=pl.BlockSpec((c, bw), lambda b, h, t: (b * nblk + t, ocb0 + h)),
        out_shape=jax.ShapeDtypeStruct(y_mix.shape, y_mix.dtype),
        input_output_aliases={9: 0},
        scratch_shapes=[pltpu.VMEM((3, SUBLANES, bw), F32), pltpu.VMEM((hps, d, d), F32)],
        compiler_params=_params("parallel", "parallel", "arbitrary"),
    )(z, z, z, z, gates, conv_w, conv_w, conv_w, norm_w.reshape(1, d), y_mix)


def _sgu_kernel(u_ref, v_ref, lng_ref, lnb_ref, ws_ref, bs_ref, o_ref, *, group):
    v = v_ref[...]
    mu = jnp.mean(v, axis=-1, keepdims=True)
    xc = v - mu
    vn = xc * lax.rsqrt(jnp.mean(xc * xc, axis=-1, keepdims=True) + EPS) * lng_ref[...] + lnb_ref[...]
    vb = vn.astype(BF16)
    ii = lax.broadcasted_iota(jnp.int32, (SGU_LEN, SGU_LEN), 0)
    jj = lax.broadcasted_iota(jnp.int32, (SGU_LEN, SGU_LEN), 1)
    chunk_bits = SGU_CHUNK.bit_length() - 1
    mask = (jj >> chunk_bits) <= (ii >> chunk_bits)
    for g in range(SGU_GROUPS):
        cols = slice(g * group, (g + 1) * group)
        ws = jnp.where(mask, ws_ref[g], 0.0).astype(BF16)
        sv = jnp.dot(ws, vb[:, cols], preferred_element_type=F32) + bs_ref[:, g:g + 1]
        o_ref[:, cols] = (u_ref[:, cols] * sv).astype(o_ref.dtype)


def spatial_gating(z, ln_g, ln_b, w_s, b_s):
    t, w2 = z.shape
    width = w2 // 2
    group = width // SGU_GROUPS
    vec = pl.BlockSpec((1, width), lambda i: (0, 0))
    return pl.pallas_call(
        functools.partial(_sgu_kernel, group=group),
        grid=(t // SGU_LEN,),
        in_specs=[pl.BlockSpec((SGU_LEN, width), lambda i: (i, 0)),
                  pl.BlockSpec((SGU_LEN, width), lambda i: (i, 1)),
                  vec, vec,
                  pl.BlockSpec((SGU_GROUPS, SGU_LEN, SGU_LEN), lambda i: (0, 0, 0)),
                  pl.BlockSpec((SGU_LEN, SGU_GROUPS), lambda i: (0, 0))],
        out_specs=pl.BlockSpec((SGU_LEN, width), lambda i: (i, 0)),
        out_shape=jax.ShapeDtypeStruct((t, width), BF16),
        compiler_params=_params("parallel"),
    )(z, z, ln_g.reshape(1, width), ln_b.reshape(1, width), w_s, b_s.T)


def _mix_pool_gdn(h, i, w_in, pool_w, pool_scale, conv_w, a_log, dt_bias, norm_w, w_out, *, batch, seq):
    n_g, group, _ = pool_w.shape
    pool_width = n_g * group
    n_heads = a_log.shape[0]
    gdn_width = n_heads * GDN_HEAD_DIM
    main = pool_width + 4 * gdn_width
    z = matmul_wstream(h, w_in, i, n_cols=main, tm=1024, tn=512)
    gates = gdn_gates(h, w_in, i, main, a_log, dt_bias, tn=1024)
    y = pool_mixer(z, pool_w.astype(BF16), pool_scale, seq=seq, out_width=pool_width + gdn_width)
    y = gated_deltanet(z, gates, conv_w, norm_w, y, batch=batch, seq=seq, col0=pool_width,
                       width=gdn_width)
    return matmul_wstream(y, w_out, i, tm=1024, tn=512)


def _mix_sgu(h, i, w_in, ln_g, ln_b, w_s, b_s, w_out):
    z = matmul_wstream(h, w_in, i, tm=1024, tn=512, act="gelu")
    s = spatial_gating(z, ln_g, ln_b, w_s, b_s)
    return matmul_wstream(s, w_out, i, tm=1024, tn=512)


def _swiglu(h, layer, w_gate, w_up, w_down):
    act, wd = ffn_up(h, w_gate, w_up, w_down, layer, tm=1024, tn=256)
    return matmul(act, wd, tm=512, tn=512)


def kernel(x, norm_mix_pre, norm_mix_post, norm_ffn_pre, norm_ffn_post, ab_w_in, pool_w, pool_scale, gdn_conv, gdn_a_log, gdn_dt_bias, gdn_norm, ab_w_out, sgu_w_in, sgu_ln_g, sgu_ln_b, sgu_w_s, sgu_b_s, sgu_w_out, ffn_w_gate, ffn_w_up, ffn_w_down):
    batch, seq, d_model = x.shape
    depth = norm_mix_pre.shape[0]
    xt = x.reshape(batch * seq, d_model)
    h = prenorm(xt, norm_mix_pre[0])
    for layer in range(depth):
        i = layer // 2
        if layer % 2 == 0:
            y = _mix_pool_gdn(h, i, ab_w_in, pool_w[i], pool_scale[i], gdn_conv[i], gdn_a_log[i],
                              gdn_dt_bias[i], gdn_norm[i], ab_w_out, batch=batch, seq=seq)
        else:
            y = _mix_sgu(h, i, sgu_w_in, sgu_ln_g[i], sgu_ln_b[i], sgu_w_s[i], sgu_b_s[i], sgu_w_out)
        xt, h = postnorm_residual(y, xt, norm_mix_post[layer], norm_ffn_pre[layer])
        y = _swiglu(h, layer, ffn_w_gate, ffn_w_up, ffn_w_down)
        nxt = norm_mix_pre[layer + 1] if layer + 1 < depth else None
        xt, h = postnorm_residual(y, xt, norm_ffn_post[layer], nxt)
    return xt.reshape(batch, seq, d_model)
```

```python
import functools

import jax
import jax.numpy as jnp
from jax import lax
from jax.experimental import pallas as pl
from jax.experimental.pallas import tpu as pltpu

F32 = jnp.float32
BF16 = jnp.bfloat16

EPS = 1e-6
POOL_WINDOWS = (2, 4, 8, 16)
POOL_HALO = 16
GDN_HEAD_DIM = 128
CONV_WIDTH = 4
GDN_BLOCK = 128
GDN_HEADS_PER_STEP = 16
SGU_GROUPS = 16
SGU_LEN = 128
SGU_CHUNK = 64
LANES = 128
SUBLANES = 8
VMEM_LIMIT_BYTES = 56 * 1024 * 1024


def _params(*semantics):
    return pltpu.CompilerParams(dimension_semantics=semantics,
                                vmem_limit_bytes=VMEM_LIMIT_BYTES)


def _sigmoid(x):
    return 1.0 / (1.0 + jnp.exp(-x))


def _silu(x):
    return x * _sigmoid(x)


def _gelu_tanh(x):
    c = 0.7978845608028654
    return 0.5 * x * (1.0 + jnp.tanh(c * (x + 0.044715 * (x * x * x))))


def _rms(x, g):
    return x * lax.rsqrt(jnp.mean(x * x, axis=-1, keepdims=True) + EPS) * g


def _prenorm_kernel(x_ref, g_ref, h_ref):
    h_ref[...] = _rms(x_ref[...], g_ref[...]).astype(h_ref.dtype)


def prenorm(x, g, *, tr=256):
    t, d = x.shape
    return pl.pallas_call(
        _prenorm_kernel,
        grid=(t // tr,),
        in_specs=[pl.BlockSpec((tr, d), lambda i: (i, 0)),
                  pl.BlockSpec((1, d), lambda i: (0, 0))],
        out_specs=pl.BlockSpec((tr, d), lambda i: (i, 0)),
        out_shape=jax.ShapeDtypeStruct((t, d), BF16),
        compiler_params=_params("parallel"),
    )(x, g.reshape(1, d))


def _postnorm_kernel(y_ref, x_ref, gpost_ref, *rest, with_next):
    xn = x_ref[...] + _rms(y_ref[...].astype(F32), gpost_ref[...])
    if with_next:
        gpre_ref, xo_ref, h_ref = rest
        h_ref[...] = _rms(xn, gpre_ref[...]).astype(h_ref.dtype)
    else:
        (xo_ref,) = rest
    xo_ref[...] = xn


def postnorm_residual(y, x, g_post, g_pre_next=None, *, tr=256):
    t, d = x.shape
    with_next = g_pre_next is not None
    row = pl.BlockSpec((tr, d), lambda i: (i, 0))
    vec = pl.BlockSpec((1, d), lambda i: (0, 0))
    in_specs = [row, row, vec]
    args = [y, x, g_post.reshape(1, d)]
    out_specs = [row]
    out_shape = [jax.ShapeDtypeStruct((t, d), F32)]
    if with_next:
        in_specs.append(vec)
        args.append(g_pre_next.reshape(1, d))
        out_specs.append(row)
        out_shape.append(jax.ShapeDtypeStruct((t, d), BF16))
    out = pl.pallas_call(
        functools.partial(_postnorm_kernel, with_next=with_next),
        grid=(t // tr,),
        in_specs=in_specs,
        out_specs=out_specs,
        out_shape=out_shape,
        compiler_params=_params("parallel"),
    )(*args)
    return (out[0], out[1]) if with_next else (out[0], None)


def _mm_stream_kernel(a_ref, w_ref, o_ref, *, act, w_transposed):
    w = w_ref[...].astype(BF16)
    contract_w = 1 if w_transposed else 0
    r = lax.dot_general(a_ref[...], w, (((1,), (contract_w,)), ((), ())),
                        preferred_element_type=F32)
    if act == "gelu":
        r = _gelu_tanh(r)
    o_ref[...] = r.astype(o_ref.dtype)


def matmul_wstream(a, w, layer, *, tm, tn, n_cols=None, out_dtype=F32, act=None,
                   w_transposed=False):
    m, kdim = a.shape
    n = w.shape[1 if w_transposed else 2] if n_cols is None else n_cols
    assert m % tm == 0 and n % tn == 0
    if w_transposed:
        wspec = pl.BlockSpec((None, tn, kdim), lambda i, j: (layer, j, 0))
    else:
        wspec = pl.BlockSpec((None, kdim, tn), lambda i, j: (layer, 0, j))
    return pl.pallas_call(
        functools.partial(_mm_stream_kernel, act=act, w_transposed=w_transposed),
        grid=(m // tm, n // tn),
        in_specs=[pl.BlockSpec((tm, kdim), lambda i, j: (i, 0)), wspec],
        out_specs=pl.BlockSpec((tm, tn), lambda i, j: (i, j)),
        out_shape=jax.ShapeDtypeStruct((m, n), out_dtype),
        compiler_params=_params("parallel", "parallel"),
    )(a, w)


def _gdn_gates_kernel(a_ref, w_ref, alog_ref, dtb_ref, o_ref, wb_ref, *, n_heads):
    @pl.when(pl.program_id(0) == 0)
    def _():
        wb_ref[...] = w_ref[...].astype(BF16)

    logits = lax.dot_general(wb_ref[...], a_ref[...], (((1,), (1,)), ((), ())),
                             preferred_element_type=F32)
    o_ref[:n_heads, :] = _sigmoid(logits[:n_heads, :])
    dl = logits[n_heads:2 * n_heads, :] + dtb_ref[...]
    softplus = jnp.maximum(dl, 0.0) + jnp.log1p(jnp.exp(-jnp.abs(dl)))
    s = -jnp.exp(alog_ref[...]) * softplus
    pos = lax.broadcasted_iota(jnp.int32, s.shape, 1) & (GDN_BLOCK - 1)
    shift = 1
    while shift < GDN_BLOCK:
        s = s + jnp.where(pos >= shift, pltpu.roll(s, shift, axis=1), 0.0)
        shift *= 2
    o_ref[n_heads:, :] = s


def gdn_gates(a, w_t, layer, row0, a_log, dt_bias, *, tn):
    n_heads = a_log.shape[0]
    m, kdim = a.shape
    assert row0 % LANES == 0 and 2 * n_heads <= LANES
    col = pl.BlockSpec((n_heads, 1), lambda i: (0, 0))
    return pl.pallas_call(
        functools.partial(_gdn_gates_kernel, n_heads=n_heads),
        grid=(m // tn,),
        in_specs=[pl.BlockSpec((tn, kdim), lambda i: (i, 0)),
                  pl.BlockSpec((None, LANES, kdim), lambda i: (layer, row0 // LANES, 0)),
                  col, col],
        out_specs=pl.BlockSpec((2 * n_heads, tn), lambda i: (0, i)),
        out_shape=jax.ShapeDtypeStruct((2 * n_heads, m), F32),
        scratch_shapes=[pltpu.VMEM((LANES, kdim), BF16)],
        compiler_params=_params("arbitrary"),
    )(a, w_t, a_log.astype(F32).reshape(n_heads, 1), dt_bias.astype(F32).reshape(n_heads, 1))


def _ffn_up_kernel(a_ref, wg_ref, wu_ref, wd_ref, o_ref, wd_bf_ref):
    @pl.when(pl.program_id(0) == 0)
    def _():
        wd_bf_ref[...] = wd_ref[...].astype(BF16)

    a = a_ref[...]
    g = jnp.dot(a, wg_ref[...].astype(BF16), preferred_element_type=F32)
    u = jnp.dot(a, wu_ref[...].astype(BF16), preferred_element_type=F32)
    o_ref[...] = (_silu(g) * u).astype(o_ref.dtype)


def ffn_up(a, wg, wu, wd, layer, *, tm, tn):
    m, kdim = a.shape
    n = wg.shape[2]
    d_out = wd.shape[2]
    assert m % tm == 0 and n % tn == 0
    nb = n // tn
    wspec = pl.BlockSpec((None, kdim, tn), lambda i, j: (layer, 0, j))
    wd_rows = lambda i, j: jnp.where(i == 0, j, nb - 1)
    return pl.pallas_call(
        _ffn_up_kernel,
        grid=(m // tm, nb),
        in_specs=[pl.BlockSpec((tm, kdim), lambda i, j: (i, 0)), wspec, wspec,
                  pl.BlockSpec((None, tn, d_out), lambda i, j: (layer, wd_rows(i, j), 0))],
        out_specs=[pl.BlockSpec((tm, tn), lambda i, j: (i, j)),
                   pl.BlockSpec((tn, d_out), lambda i, j: (wd_rows(i, j), 0))],
        out_shape=[jax.ShapeDtypeStruct((m, n), BF16), jax.ShapeDtypeStruct((n, d_out), BF16)],
        compiler_params=_params("arbitrary", "arbitrary"),
    )(a, wg, wu, wd)


def _pool_kernel(halo_ref, x_ref, w_ref, scale_ref, o_ref, *, tb, seq, group):
    start = (pl.program_id(0) * tb) % seq
    keep_halo = (start > 0).astype(F32)
    pos = start + 1 + lax.broadcasted_iota(jnp.int32, (tb, 1), 0)
    for gi, win in enumerate(POOL_WINDOWS):
        cols = slice(gi * group, (gi + 1) * group)
        x = x_ref[:, cols]
        s = jnp.concatenate([halo_ref[:, cols] * keep_halo, x], axis=0)
        shift = 1
        while shift < win:
            s = s + pltpu.roll(s, shift, axis=0)
            shift *= 2
        cnt = jnp.minimum(pos, win).astype(F32)
        y = s[POOL_HALO:, :] / cnt - x
        r = jnp.dot(y.astype(BF16), w_ref[gi], preferred_element_type=F32)
        o_ref[:, cols] = (r * scale_ref[:, cols]).astype(o_ref.dtype)
    width = len(POOL_WINDOWS) * group
    o_ref[:, width:] = jnp.zeros((tb, o_ref.shape[1] - width), o_ref.dtype)


def pool_mixer(z, w_grp, scale, *, seq, out_width, tb=256):
    t = z.shape[0]
    n_g, group, _ = w_grp.shape
    width = n_g * group
    hb = tb // POOL_HALO
    return pl.pallas_call(
        functools.partial(_pool_kernel, tb=tb, seq=seq, group=group),
        grid=(t // tb,),
        in_specs=[pl.BlockSpec((POOL_HALO, width), lambda i: (jnp.maximum(i * hb - 1, 0), 0)),
                  pl.BlockSpec((tb, width), lambda i: (i, 0)),
                  pl.BlockSpec((n_g, group, group), lambda i: (0, 0, 0)),
                  pl.BlockSpec((1, width), lambda i: (0, 0))],
        out_specs=pl.BlockSpec((tb, out_width), lambda i: (i, 0)),
        out_shape=jax.ShapeDtypeStruct((t, out_width), BF16),
        compiler_params=_params("parallel"),
    )(z, z, w_grp, scale.reshape(1, width))


def _gdn_kernel(q_ref, k_ref, v_ref, gate_ref, gates_ref, cq_ref, ck_ref, cv_ref, nw_ref,
                y_hbm_ref, o_ref, halo_ref, state_ref, *, hps, n_heads):
    del y_hbm_ref
    c = GDN_BLOCK
    d = GDN_HEAD_DIM
    halo = SUBLANES
    heads = range(hps)
    t = pl.program_id(2)
    h0 = pl.program_id(1) * hps

    @pl.when(t == 0)
    def _():
        halo_ref[...] = jnp.zeros_like(halo_ref)
        state_ref[...] = jnp.zeros_like(state_ref)

    def conv_silu(x_ref, idx, cw_ref):
        cur = x_ref[...]
        ext = jnp.concatenate([halo_ref[idx], cur], axis=0)
        w = cw_ref[...]
        acc = cur * w[CONV_WIDTH - 1:CONV_WIDTH, :]
        for back in range(1, CONV_WIDTH):
            tap = pltpu.roll(ext, back, axis=0)[halo:, :]
            acc = acc + tap * w[CONV_WIDTH - 1 - back:CONV_WIDTH - back, :]
        halo_ref[idx] = cur[c - halo:, :]
        return _silu(acc)

    qs = conv_silu(q_ref, 0, cq_ref)
    ks = conv_silu(k_ref, 1, ck_ref)
    vs = conv_silu(v_ref, 2, cv_ref)

    ii = lax.broadcasted_iota(jnp.int32, (c, c), 0)
    jj = lax.broadcasted_iota(jnp.int32, (c, c), 1)
    causal = ii >= jj
    strict = ii > jj

    def dot(a, b):
        return jnp.dot(a.astype(BF16), b.astype(BF16), preferred_element_type=F32)

    def dot_nt(a, b):
        return lax.dot_general(a.astype(BF16), b.astype(BF16), (((1,), (1,)), ((), ())),
                               preferred_element_type=F32)

    def l2norm(x):
        return x * lax.rsqrt(jnp.sum(x * x, axis=-1, keepdims=True) + EPS)

    q = [l2norm(qs[:, j * d:(j + 1) * d]) * (d ** -0.5) for j in heads]
    k = [l2norm(ks[:, j * d:(j + 1) * d]) for j in heads]
    v = [vs[:, j * d:(j + 1) * d] for j in heads]

    beta_rows = gates_ref[pl.ds(h0, hps), :]
    gc_rows = gates_ref[pl.ds(n_heads + h0, hps), :]
    beta = [jnp.broadcast_to(beta_rows[j:j + 1, :], (c, c)).T for j in heads]
    gc_t = [jnp.broadcast_to(gc_rows[j:j + 1, :], (c, c)) for j in heads]
    gc = [g.T for g in gc_t]
    decay = [jnp.exp(jnp.where(causal, gc[j] - gc_t[j], -jnp.inf)) for j in heads]
    gamma = [jnp.exp(g) for g in gc]
    gc_last = [g[c - 1:c, :] for g in gc]
    kb = [k[j] * beta[j] for j in heads]
    lmat = [jnp.where(strict, dot_nt(kb[j], k[j]) * decay[j], 0.0) for j in heads]
    attn = [dot_nt(q[j], k[j]) * decay[j] for j in heads]

    n = [-m for m in lmat]
    p = lmat
    width = 2
    while width < c:
        p = [dot(m, m) for m in p]
        n = [n[j] + p[j] + dot(n[j], p[j]) for j in heads]
        width *= 2
    rhs = [jnp.concatenate([kb[j] * gamma[j], v[j] * beta[j]], axis=1) for j in heads]
    sol = [rhs[j] + dot(n[j], rhs[j]) for j in heads]
    qd = [q[j] * gamma[j] for j in heads]
    kd_t = [(k[j] * jnp.exp(gc_last[j] - gc[j])).T for j in heads]

    state = [state_ref[j] for j in heads]
    v_new = [sol[j][:, d:] - dot(sol[j][:, :d], state[j]) for j in heads]
    o = [dot(qd[j], state[j]) + dot(attn[j], v_new[j]) for j in heads]
    for j in heads:
        state_ref[j] = jnp.exp(gc_last[j]) * state[j] + dot(kd_t[j], v_new[j])
    for j in heads:
        gate = gate_ref[:, j * d:(j + 1) * d]
        o_ref[:, j * d:(j + 1) * d] = (_rms(o[j], nw_ref[...]) * _silu(gate)).astype(o_ref.dtype)


def gated_deltanet(z, gates, conv_w, norm_w, y_mix, *, batch, seq, col0, width):
    d = GDN_HEAD_DIM
    c = GDN_BLOCK
    hps = GDN_HEADS_PER_STEP
    n_heads = width // d
    bw = hps * d
    nblk = seq // c
    cb0 = col0 // bw
    wb = width // bw
    ocb0 = (y_mix.shape[1] - width) // bw

    def zspec(part):
        return pl.BlockSpec((c, bw), lambda b, h, t: (b * nblk + t, cb0 + part * wb + h))

    def cspec(part):
        return pl.BlockSpec((CONV_WIDTH, bw), lambda b, h, t: (0, part * wb + h))

    return pl.pallas_call(
        functools.partial(_gdn_kernel, hps=hps, n_heads=n_heads),
        grid=(batch, n_heads // hps, nblk),
        in_specs=[zspec(0), zspec(1), zspec(2), zspec(3),
                  pl.BlockSpec((2 * n_heads, c), lambda b, h, t: (0, b * nblk + t)),
                  cspec(0), cspec(1), cspec(2),
                  pl.BlockSpec((1, d), lambda b, h, t: (0, 0)),
                  pl.BlockSpec(memory_space=pl.ANY)],
        out_specs=pl.BlockSpec((c, bw), lambda b, h, t: (b * nblk + t, ocb0 + h)),
        out_shape=jax.ShapeDtypeStruct(y_mix.shape, y_mix.dtype),
        input_output_aliases={9: 0},
        scratch_shapes=[pltpu.VMEM((3, SUBLANES, bw), F32), pltpu.VMEM((hps, d, d), F32)],
        compiler_params=_params("parallel", "parallel", "arbitrary"),
    )(z, z, z, z, gates, conv_w, conv_w, conv_w, norm_w.reshape(1, d), y_mix)


def _sgu_kernel(u_ref, v_ref, lng_ref, lnb_ref, ws_ref, bs_ref, o_ref, *, group):
    v = v_ref[...]
    mu = jnp.mean(v, axis=-1, keepdims=True)
    xc = v - mu
    vn = xc * lax.rsqrt(jnp.mean(xc * xc, axis=-1, keepdims=True) + EPS) * lng_ref[...] + lnb_ref[...]
    vb = vn.astype(BF16)
    ii = lax.broadcasted_iota(jnp.int32, (SGU_LEN, SGU_LEN), 0)
    jj = lax.broadcasted_iota(jnp.int32, (SGU_LEN, SGU_LEN), 1)
    chunk_bits = SGU_CHUNK.bit_length() - 1
    mask = (jj >> chunk_bits) <= (ii >> chunk_bits)
    for g in range(SGU_GROUPS):
        cols = slice(g * group, (g + 1) * group)
        ws = jnp.where(mask, ws_ref[g], 0.0).astype(BF16)
        sv = jnp.dot(ws, vb[:, cols], preferred_element_type=F32) + bs_ref[:, g:g + 1]
        o_ref[:, cols] = (u_ref[:, cols] * sv).astype(o_ref.dtype)


def spatial_gating(z, ln_g, ln_b, w_s, b_s):
    t, w2 = z.shape
    width = w2 // 2
    group = width // SGU_GROUPS
    vec = pl.BlockSpec((1, width), lambda i: (0, 0))
    return pl.pallas_call(
        functools.partial(_sgu_kernel, group=group),
        grid=(t // SGU_LEN,),
        in_specs=[pl.BlockSpec((SGU_LEN, width), lambda i: (i, 0)),
                  pl.BlockSpec((SGU_LEN, width), lambda i: (i, 1)),
                  vec, vec,
                  pl.BlockSpec((SGU_GROUPS, SGU_LEN, SGU_LEN), lambda i: (0, 0, 0)),
                  pl.BlockSpec((SGU_LEN, SGU_GROUPS), lambda i: (0, 0))],
        out_specs=pl.BlockSpec((SGU_LEN, width), lambda i: (i, 0)),
        out_shape=jax.ShapeDtypeStruct((t, width), BF16),
        compiler_params=_params("parallel"),
    )(z, z, ln_g.reshape(1, width), ln_b.reshape(1, width), w_s, b_s.T)


def _mix_pool_gdn(h, i, w_in, pool_w, pool_scale, conv_w, a_log, dt_bias, norm_w, w_out, *, batch, seq):
    n_g, group, _ = pool_w.shape
    pool_width = n_g * group
    n_heads = a_log.shape[0]
    gdn_width = n_heads * GDN_HEAD_DIM
    main = pool_width + 4 * gdn_width
    w_in_t = jnp.swapaxes(w_in, 1, 2)
    z = matmul_wstream(h, w_in_t, i, n_cols=main, tm=1024, tn=512, w_transposed=True)
    gates = gdn_gates(h, w_in_t, i, main, a_log, dt_bias, tn=1024)
    y = pool_mixer(z, pool_w.astype(BF16), pool_scale, seq=seq, out_width=pool_width + gdn_width)
    y = gated_deltanet(z, gates, conv_w, norm_w, y, batch=batch, seq=seq, col0=pool_width,
                       width=gdn_width)
    return matmul_wstream(y, w_out, i, tm=1024, tn=512, out_dtype=BF16)


def _mix_sgu(h, i, w_in, ln_g, ln_b, w_s, b_s, w_out):
    z = matmul_wstream(h, w_in, i, tm=1024, tn=512, act="gelu")
    s = spatial_gating(z, ln_g, ln_b, w_s, b_s)
    return matmul_wstream(s, w_out, i, tm=1024, tn=512, out_dtype=BF16)


def _swiglu(h, layer, w_gate, w_up, w_down):
    act, wd = ffn_up(h, w_gate, w_up, w_down, layer, tm=1024, tn=256)
    return matmul_wstream(act, wd[None], 0, tm=512, tn=512, out_dtype=BF16)


def kernel(x, norm_mix_pre, norm_mix_post, norm_ffn_pre, norm_ffn_post, ab_w_in, pool_w, pool_scale, gdn_conv, gdn_a_log, gdn_dt_bias, gdn_norm, ab_w_out, sgu_w_in, sgu_ln_g, sgu_ln_b, sgu_w_s, sgu_b_s, sgu_w_out, ffn_w_gate, ffn_w_up, ffn_w_down):
    batch, seq, d_model = x.shape
    depth = norm_mix_pre.shape[0]
    xt = x.reshape(batch * seq, d_model)
    h = prenorm(xt, norm_mix_pre[0])
    for layer in range(depth):
        i = layer // 2
        if layer % 2 == 0:
            y = _mix_pool_gdn(h, i, ab_w_in, pool_w[i], pool_scale[i], gdn_conv[i], gdn_a_log[i],
                              gdn_dt_bias[i], gdn_norm[i], ab_w_out, batch=batch, seq=seq)
        else:
            y = _mix_sgu(h, i, sgu_w_in, sgu_ln_g[i], sgu_ln_b[i], sgu_w_s[i], sgu_b_s[i], sgu_w_out)
        xt, h = postnorm_residual(y, xt, norm_mix_post[layer], norm_ffn_pre[layer])
        y = _swiglu(h, layer, ffn_w_gate, ffn_w_up, ffn_w_down)
        nxt = norm_mix_pre[layer + 1] if layer + 1 < depth else None
        xt, h = postnorm_residual(y, xt, norm_ffn_post[layer], nxt)
    return xt.reshape(batch, seq, d_model)
```

```python
import functools

import jax
import jax.numpy as jnp
from jax import lax
from jax.experimental import pallas as pl
from jax.experimental.pallas import tpu as pltpu

F32 = jnp.float32
BF16 = jnp.bfloat16

EPS = 1e-6
POOL_WINDOWS = (2, 4, 8, 16)
POOL_HALO = 16
GDN_HEAD_DIM = 128
CONV_WIDTH = 4
GDN_BLOCK = 128
GDN_HEADS_PER_STEP = 16
SGU_GROUPS = 16
SGU_LEN = 128
SGU_CHUNK = 64
LANES = 128
SUBLANES = 8
VMEM_LIMIT_BYTES = 56 * 1024 * 1024


def _params(*semantics):
    return pltpu.CompilerParams(dimension_semantics=semantics,
                                vmem_limit_bytes=VMEM_LIMIT_BYTES)


def _sigmoid(x):
    return 1.0 / (1.0 + jnp.exp(-x))


def _silu(x):
    return x * _sigmoid(x)


def _gelu_tanh(x):
    c = 0.7978845608028654
    return 0.5 * x * (1.0 + jnp.tanh(c * (x + 0.044715 * (x * x * x))))


def _rms(x, g):
    return x * lax.rsqrt(jnp.mean(x * x, axis=-1, keepdims=True) + EPS) * g


def _prenorm_kernel(x_ref, g_ref, h_ref):
    h_ref[...] = _rms(x_ref[...], g_ref[...]).astype(h_ref.dtype)


def prenorm(x, g, *, tr=256):
    t, d = x.shape
    return pl.pallas_call(
        _prenorm_kernel,
        grid=(t // tr,),
        in_specs=[pl.BlockSpec((tr, d), lambda i: (i, 0)),
                  pl.BlockSpec((1, d), lambda i: (0, 0))],
        out_specs=pl.BlockSpec((tr, d), lambda i: (i, 0)),
        out_shape=jax.ShapeDtypeStruct((t, d), BF16),
        compiler_params=_params("parallel"),
    )(x, g.reshape(1, d))


def _postnorm_kernel(y_ref, x_ref, gpost_ref, *rest, with_next):
    xn = x_ref[...] + _rms(y_ref[...].astype(F32), gpost_ref[...])
    if with_next:
        gpre_ref, xo_ref, h_ref = rest
        h_ref[...] = _rms(xn, gpre_ref[...]).astype(h_ref.dtype)
    else:
        (xo_ref,) = rest
    xo_ref[...] = xn


def postnorm_residual(y, x, g_post, g_pre_next=None, *, tr=256):
    t, d = x.shape
    with_next = g_pre_next is not None
    row = pl.BlockSpec((tr, d), lambda i: (i, 0))
    vec = pl.BlockSpec((1, d), lambda i: (0, 0))
    in_specs = [row, row, vec]
    args = [y, x, g_post.reshape(1, d)]
    out_specs = [row]
    out_shape = [jax.ShapeDtypeStruct((t, d), F32)]
    if with_next:
        in_specs.append(vec)
        args.append(g_pre_next.reshape(1, d))
        out_specs.append(row)
        out_shape.append(jax.ShapeDtypeStruct((t, d), BF16))
    out = pl.pallas_call(
        functools.partial(_postnorm_kernel, with_next=with_next),
        grid=(t // tr,),
        in_specs=in_specs,
        out_specs=out_specs,
        out_shape=out_shape,
        compiler_params=_params("parallel"),
    )(*args)
    return (out[0], out[1]) if with_next else (out[0], None)


def _mm_stream_kernel(a_ref, w_ref, o_ref, *, act, w_transposed):
    w = w_ref[...].astype(BF16)
    contract_w = 1 if w_transposed else 0
    r = lax.dot_general(a_ref[...], w, (((1,), (contract_w,)), ((), ())),
                        preferred_element_type=F32)
    if act == "gelu":
        r = _gelu_tanh(r)
    o_ref[...] = r.astype(o_ref.dtype)


def matmul_wstream(a, w, layer, *, tm, tn, n_cols=None, out_dtype=F32, act=None,
                   w_transposed=False):
    m, kdim = a.shape
    n = w.shape[1 if w_transposed else 2] if n_cols is None else n_cols
    assert m % tm == 0 and n % tn == 0
    if w_transposed:
        wspec = pl.BlockSpec((None, tn, kdim), lambda i, j: (layer, j, 0))
    else:
        wspec = pl.BlockSpec((None, kdim, tn), lambda i, j: (layer, 0, j))
    return pl.pallas_call(
        functools.partial(_mm_stream_kernel, act=act, w_transposed=w_transposed),
        grid=(m // tm, n // tn),
        in_specs=[pl.BlockSpec((tm, kdim), lambda i, j: (i, 0)), wspec],
        out_specs=pl.BlockSpec((tm, tn), lambda i, j: (i, j)),
        out_shape=jax.ShapeDtypeStruct((m, n), out_dtype),
        compiler_params=_params("parallel", "parallel"),
    )(a, w)


def _gdn_gates_kernel(a_ref, w_ref, alog_ref, dtb_ref, o_ref, wb_ref, *, n_heads):
    @pl.when(pl.program_id(0) == 0)
    def _():
        wb_ref[...] = w_ref[...].astype(BF16)

    logits = lax.dot_general(wb_ref[...], a_ref[...], (((1,), (1,)), ((), ())),
                             preferred_element_type=F32)
    o_ref[:n_heads, :] = _sigmoid(logits[:n_heads, :])
    dl = logits[n_heads:2 * n_heads, :] + dtb_ref[...]
    softplus = jnp.maximum(dl, 0.0) + jnp.log1p(jnp.exp(-jnp.abs(dl)))
    s = -jnp.exp(alog_ref[...]) * softplus
    pos = lax.broadcasted_iota(jnp.int32, s.shape, 1) & (GDN_BLOCK - 1)
    shift = 1
    while shift < GDN_BLOCK:
        s = s + jnp.where(pos >= shift, pltpu.roll(s, shift, axis=1), 0.0)
        shift *= 2
    o_ref[n_heads:, :] = s


def gdn_gates(a, w_t, layer, row0, a_log, dt_bias, *, tn):
    n_heads = a_log.shape[0]
    m, kdim = a.shape
    assert row0 % LANES == 0 and 2 * n_heads <= LANES
    col = pl.BlockSpec((n_heads, 1), lambda i: (0, 0))
    return pl.pallas_call(
        functools.partial(_gdn_gates_kernel, n_heads=n_heads),
        grid=(m // tn,),
        in_specs=[pl.BlockSpec((tn, kdim), lambda i: (i, 0)),
                  pl.BlockSpec((None, LANES, kdim), lambda i: (layer, row0 // LANES, 0)),
                  col, col],
        out_specs=pl.BlockSpec((2 * n_heads, tn), lambda i: (0, i)),
        out_shape=jax.ShapeDtypeStruct((2 * n_heads, m), F32),
        scratch_shapes=[pltpu.VMEM((LANES, kdim), BF16)],
        compiler_params=_params("arbitrary"),
    )(a, w_t, a_log.astype(F32).reshape(n_heads, 1), dt_bias.astype(F32).reshape(n_heads, 1))


def _ffn_up_kernel(a_ref, wg_ref, wu_ref, wd_ref, o_ref, wd_bf_ref):
    @pl.when(pl.program_id(0) == 0)
    def _():
        wd_bf_ref[...] = wd_ref[...].astype(BF16)

    a = a_ref[...]
    g = jnp.dot(a, wg_ref[...].astype(BF16), preferred_element_type=F32)
    u = jnp.dot(a, wu_ref[...].astype(BF16), preferred_element_type=F32)
    o_ref[...] = (_silu(g) * u).astype(o_ref.dtype)


def ffn_up(a, wg, wu, wd, layer, *, tm, tn):
    m, kdim = a.shape
    n = wg.shape[2]
    d_out = wd.shape[2]
    assert m % tm == 0 and n % tn == 0
    nb = n // tn
    wspec = pl.BlockSpec((None, kdim, tn), lambda i, j: (layer, 0, j))
    wd_rows = lambda i, j: jnp.where(i == 0, j, nb - 1)
    return pl.pallas_call(
        _ffn_up_kernel,
        grid=(m // tm, nb),
        in_specs=[pl.BlockSpec((tm, kdim), lambda i, j: (i, 0)), wspec, wspec,
                  pl.BlockSpec((None, tn, d_out), lambda i, j: (layer, wd_rows(i, j), 0))],
        out_specs=[pl.BlockSpec((tm, tn), lambda i, j: (i, j)),
                   pl.BlockSpec((tn, d_out), lambda i, j: (wd_rows(i, j), 0))],
        out_shape=[jax.ShapeDtypeStruct((m, n), BF16), jax.ShapeDtypeStruct((n, d_out), BF16)],
        compiler_params=_params("arbitrary", "arbitrary"),
    )(a, wg, wu, wd)


def _pool_kernel(halo_ref, x_ref, w_ref, scale_ref, o_ref, *, tb, seq, group):
    start = (pl.program_id(0) * tb) % seq
    keep_halo = (start > 0).astype(F32)
    pos = start + 1 + lax.broadcasted_iota(jnp.int32, (tb, 1), 0)
    for gi, win in enumerate(POOL_WINDOWS):
        cols = slice(gi * group, (gi + 1) * group)
        x = x_ref[:, cols].astype(F32)
        s = jnp.concatenate([halo_ref[:, cols].astype(F32) * keep_halo, x], axis=0)
        shift = 1
        while shift < win:
            s = s + pltpu.roll(s, shift, axis=0)
            shift *= 2
        cnt = jnp.minimum(pos, win).astype(F32)
        y = s[POOL_HALO:, :] / cnt - x
        r = jnp.dot(y.astype(BF16), w_ref[gi], preferred_element_type=F32)
        o_ref[:, cols] = (r * scale_ref[:, cols]).astype(o_ref.dtype)
    width = len(POOL_WINDOWS) * group
    o_ref[:, width:] = jnp.zeros((tb, o_ref.shape[1] - width), o_ref.dtype)


def pool_mixer(z, w_grp, scale, *, seq, out_width, tb=256):
    t = z.shape[0]
    n_g, group, _ = w_grp.shape
    width = n_g * group
    hb = tb // POOL_HALO
    return pl.pallas_call(
        functools.partial(_pool_kernel, tb=tb, seq=seq, group=group),
        grid=(t // tb,),
        in_specs=[pl.BlockSpec((POOL_HALO, width), lambda i: (jnp.maximum(i * hb - 1, 0), 0)),
                  pl.BlockSpec((tb, width), lambda i: (i, 0)),
                  pl.BlockSpec((n_g, group, group), lambda i: (0, 0, 0)),
                  pl.BlockSpec((1, width), lambda i: (0, 0))],
        out_specs=pl.BlockSpec((tb, out_width), lambda i: (i, 0)),
        out_shape=jax.ShapeDtypeStruct((t, out_width), BF16),
        compiler_params=_params("parallel"),
    )(z, z, w_grp, scale.reshape(1, width))


def _gdn_kernel(q_ref, k_ref, v_ref, gate_ref, gates_ref, cq_ref, ck_ref, cv_ref, nw_ref,
                y_hbm_ref, o_ref, halo_ref, state_ref, *, hps, n_heads):
    del y_hbm_ref
    c = GDN_BLOCK
    d = GDN_HEAD_DIM
    halo = SUBLANES
    heads = range(hps)
    t = pl.program_id(2)
    h0 = pl.program_id(1) * hps

    @pl.when(t == 0)
    def _():
        halo_ref[...] = jnp.zeros_like(halo_ref)
        state_ref[...] = jnp.zeros_like(state_ref)

    width = hps * d
    sub = lax.broadcasted_iota(jnp.int32, (1, halo, width), 1)

    def conv_silu(x_ref, idx, cw_ref):
        cur = x_ref[...].astype(F32)
        tiles = jnp.concatenate([halo_ref[idx], cur], axis=0).reshape(c // halo + 1, halo, width)
        w = cw_ref[...]
        acc = cur * w[CONV_WIDTH - 1:CONV_WIDTH, :]
        for back in range(1, CONV_WIDTH):
            rot = pltpu.roll(tiles, back, axis=1)
            tap = jnp.where(sub < back, rot[:-1], rot[1:]).reshape(c, width)
            acc = acc + tap * w[CONV_WIDTH - 1 - back:CONV_WIDTH - back, :]
        halo_ref[idx] = cur[c - halo:, :]
        return _silu(acc)

    qs = conv_silu(q_ref, 0, cq_ref)
    ks = conv_silu(k_ref, 1, ck_ref)
    vs = conv_silu(v_ref, 2, cv_ref)

    ii = lax.broadcasted_iota(jnp.int32, (c, c), 0)
    jj = lax.broadcasted_iota(jnp.int32, (c, c), 1)
    causal = ii >= jj
    strict = ii > jj

    def bf(a):
        return a.astype(BF16)

    def mm(a, b):
        return jnp.dot(a, b, preferred_element_type=F32)

    def mm_nt(a, b):
        return lax.dot_general(a, b, (((1,), (1,)), ((), ())), preferred_element_type=F32)

    def l2norm(x):
        return x * lax.rsqrt(jnp.sum(x * x, axis=-1, keepdims=True) + EPS)

    q = [l2norm(qs[:, j * d:(j + 1) * d]) * (d ** -0.5) for j in heads]
    k = [l2norm(ks[:, j * d:(j + 1) * d]) for j in heads]
    v = [vs[:, j * d:(j + 1) * d] for j in heads]
    k_bf = [bf(x) for x in k]

    beta_rows = gates_ref[pl.ds(h0, hps), :]
    gc_rows = gates_ref[pl.ds(n_heads + h0, hps), :]
    beta = [jnp.broadcast_to(beta_rows[j:j + 1, :], (c, c)).T for j in heads]
    gc_t = [jnp.broadcast_to(gc_rows[j:j + 1, :], (c, c)) for j in heads]
    gc = [g.T for g in gc_t]
    decay = [jnp.exp(jnp.where(causal, gc[j] - gc_t[j], -jnp.inf)) for j in heads]
    gamma = [jnp.exp(g) for g in gc]
    gc_last = [g[c - 1:c, :] for g in gc]
    kb = [k[j] * beta[j] for j in heads]
    lmat = [jnp.where(strict, mm_nt(bf(kb[j]), k_bf[j]) * decay[j], 0.0) for j in heads]
    attn = [mm_nt(bf(q[j]), k_bf[j]) * decay[j] for j in heads]

    n = [-m for m in lmat]
    p_bf = [bf(m) for m in lmat]
    span = 2
    while span < c:
        p = [mm(m, m) for m in p_bf]
        p_bf = [bf(m) for m in p]
        n = [n[j] + p[j] + mm(bf(n[j]), p_bf[j]) for j in heads]
        span *= 2
    rhs = [jnp.concatenate([kb[j] * gamma[j], v[j] * beta[j]], axis=1) for j in heads]
    sol = [rhs[j] + mm(bf(n[j]), bf(rhs[j])) for j in heads]
    qd = [q[j] * gamma[j] for j in heads]
    kd_t = [(k[j] * jnp.exp(gc_last[j] - gc[j])).T for j in heads]

    state = [state_ref[j] for j in heads]
    state_bf = [bf(m) for m in state]
    v_new = [sol[j][:, d:] - mm(bf(sol[j][:, :d]), state_bf[j]) for j in heads]
    v_new_bf = [bf(m) for m in v_new]
    o = [mm(bf(qd[j]), state_bf[j]) + mm(bf(attn[j]), v_new_bf[j]) for j in heads]
    for j in heads:
        state_ref[j] = jnp.exp(gc_last[j]) * state[j] + mm(bf(kd_t[j]), v_new_bf[j])
    for j in heads:
        gate = gate_ref[:, j * d:(j + 1) * d].astype(F32)
        o_ref[:, j * d:(j + 1) * d] = (_rms(o[j], nw_ref[...]) * _silu(gate)).astype(o_ref.dtype)


def gated_deltanet(z, gates, conv_w, norm_w, y_mix, *, batch, seq, col0, width):
    d = GDN_HEAD_DIM
    c = GDN_BLOCK
    hps = GDN_HEADS_PER_STEP
    n_heads = width // d
    bw = hps * d
    nblk = seq // c
    cb0 = col0 // bw
    wb = width // bw
    ocb0 = (y_mix.shape[1] - width) // bw

    def zspec(part):
        return pl.BlockSpec((c, bw), lambda b, h, t: (b * nblk + t, cb0 + part * wb + h))

    def cspec(part):
        return pl.BlockSpec((CONV_WIDTH, bw), lambda b, h, t: (0, part * wb + h))

    return pl.pallas_call(
        functools.partial(_gdn_kernel, hps=hps, n_heads=n_heads),
        grid=(batch, n_heads // hps, nblk),
        in_specs=[zspec(0), zspec(1), zspec(2), zspec(3),
                  pl.BlockSpec((2 * n_heads, c), lambda b, h, t: (0, b * nblk + t)),
                  cspec(0), cspec(1), cspec(2),
                  pl.BlockSpec((1, d), lambda b, h, t: (0, 0)),
                  pl.BlockSpec(memory_space=pl.ANY)],
        out_specs=pl.BlockSpec((c, bw), lambda b, h, t: (b * nblk + t, ocb0 + h)),
        out_shape=jax.ShapeDtypeStruct(y_mix.shape, y_mix.dtype),
        input_output_aliases={9: 0},
        scratch_shapes=[pltpu.VMEM((3, SUBLANES, bw), F32), pltpu.VMEM((hps, d, d), F32)],
        compiler_params=_params("parallel", "parallel", "arbitrary"),
    )(z, z, z, z, gates, conv_w, conv_w, conv_w, norm_w.reshape(1, d), y_mix)


def _sgu_kernel(u_ref, v_ref, lng_ref, lnb_ref, ws_ref, bs_ref, o_ref, *, group):
    v = v_ref[...].astype(F32)
    mu = jnp.mean(v, axis=-1, keepdims=True)
    xc = v - mu
    vn = xc * lax.rsqrt(jnp.mean(xc * xc, axis=-1, keepdims=True) + EPS) * lng_ref[...] + lnb_ref[...]
    vb = vn.astype(BF16)
    ii = lax.broadcasted_iota(jnp.int32, (SGU_LEN, SGU_LEN), 0)
    jj = lax.broadcasted_iota(jnp.int32, (SGU_LEN, SGU_LEN), 1)
    chunk_bits = SGU_CHUNK.bit_length() - 1
    mask = (jj >> chunk_bits) <= (ii >> chunk_bits)
    for g in range(SGU_GROUPS):
        cols = slice(g * group, (g + 1) * group)
        ws = jnp.where(mask, ws_ref[g], 0.0).astype(BF16)
        sv = jnp.dot(ws, vb[:, cols], preferred_element_type=F32) + bs_ref[:, g:g + 1]
        o_ref[:, cols] = (u_ref[:, cols].astype(F32) * sv).astype(o_ref.dtype)


def spatial_gating(z, ln_g, ln_b, w_s, b_s):
    t, w2 = z.shape
    width = w2 // 2
    group = width // SGU_GROUPS
    vec = pl.BlockSpec((1, width), lambda i: (0, 0))
    return pl.pallas_call(
        functools.partial(_sgu_kernel, group=group),
        grid=(t // SGU_LEN,),
        in_specs=[pl.BlockSpec((SGU_LEN, width), lambda i: (i, 0)),
                  pl.BlockSpec((SGU_LEN, width), lambda i: (i, 1)),
                  vec, vec,
                  pl.BlockSpec((SGU_GROUPS, SGU_LEN, SGU_LEN), lambda i: (0, 0, 0)),
                  pl.BlockSpec((SGU_LEN, SGU_GROUPS), lambda i: (0, 0))],
        out_specs=pl.BlockSpec((SGU_LEN, width), lambda i: (i, 0)),
        out_shape=jax.ShapeDtypeStruct((t, width), BF16),
        compiler_params=_params("parallel"),
    )(z, z, ln_g.reshape(1, width), ln_b.reshape(1, width), w_s, b_s.T)


def _mix_pool_gdn(h, i, w_in, pool_w, pool_scale, conv_w, a_log, dt_bias, norm_w, w_out, *, batch, seq):
    n_g, group, _ = pool_w.shape
    pool_width = n_g * group
    n_heads = a_log.shape[0]
    gdn_width = n_heads * GDN_HEAD_DIM
    main = pool_width + 4 * gdn_width
    w_in_t = jnp.swapaxes(w_in, 1, 2)
    z = matmul_wstream(h, w_in_t, i, n_cols=main, tm=1024, tn=512, out_dtype=BF16,
                       w_transposed=True)
    gates = gdn_gates(h, w_in_t, i, main, a_log, dt_bias, tn=1024)
    y = pool_mixer(z, pool_w.astype(BF16), pool_scale, seq=seq, out_width=pool_width + gdn_width)
    y = gated_deltanet(z, gates, conv_w, norm_w, y, batch=batch, seq=seq, col0=pool_width,
                       width=gdn_width)
    return matmul_wstream(y, w_out, i, tm=1024, tn=512, out_dtype=BF16)


def _mix_sgu(h, i, w_in, ln_g, ln_b, w_s, b_s, w_out):
    z = matmul_wstream(h, w_in, i, tm=1024, tn=512, out_dtype=BF16, act="gelu")
    s = spatial_gating(z, ln_g, ln_b, w_s, b_s)
    return matmul_wstream(s, w_out, i, tm=1024, tn=512, out_dtype=BF16)


def _swiglu(h, layer, w_gate, w_up, w_down):
    act, wd = ffn_up(h, w_gate, w_up, w_down, layer, tm=1024, tn=256)
    return matmul_wstream(act, wd[None], 0, tm=512, tn=512, out_dtype=BF16)


def kernel(x, norm_mix_pre, norm_mix_post, norm_ffn_pre, norm_ffn_post, ab_w_in, pool_w, pool_scale, gdn_conv, gdn_a_log, gdn_dt_bias, gdn_norm, ab_w_out, sgu_w_in, sgu_ln_g, sgu_ln_b, sgu_w_s, sgu_b_s, sgu_w_out, ffn_w_gate, ffn_w_up, ffn_w_down):
    batch, seq, d_model = x.shape
    depth = norm_mix_pre.shape[0]
    xt = x.reshape(batch * seq, d_model)
    h = prenorm(xt, norm_mix_pre[0])
    for layer in range(depth):
        i = layer // 2
        if layer % 2 == 0:
            y = _mix_pool_gdn(h, i, ab_w_in, pool_w[i], pool_scale[i], gdn_conv[i], gdn_a_log[i],
                              gdn_dt_bias[i], gdn_norm[i], ab_w_out, batch=batch, seq=seq)
        else:
            y = _mix_sgu(h, i, sgu_w_in, sgu_ln_g[i], sgu_ln_b[i], sgu_w_s[i], sgu_b_s[i], sgu_w_out)
        xt, h = postnorm_residual(y, xt, norm_mix_post[layer], norm_ffn_pre[layer])
        y = _swiglu(h, layer, ffn_w_gate, ffn_w_up, ffn_w_down)
        nxt = norm_mix_pre[layer + 1] if layer + 1 < depth else None
        xt, h = postnorm_residual(y, xt, norm_ffn_post[layer], nxt)
    return xt.reshape(batch, seq, d_model)
```

```python
import functools

import jax
import jax.numpy as jnp
from jax import lax
from jax.experimental import pallas as pl
from jax.experimental.pallas import tpu as pltpu

F32 = jnp.float32
BF16 = jnp.bfloat16

EPS = 1e-6
POOL_WINDOWS = (2, 4, 8, 16)
POOL_HALO = 16
GDN_HEAD_DIM = 128
CONV_WIDTH = 4
GDN_BLOCK = 128
GDN_HEADS_PER_STEP = 16
SGU_GROUPS = 16
SGU_LEN = 128
SGU_CHUNK = 64
LANES = 128
SUBLANES = 8
VMEM_LIMIT_BYTES = 62 * 1024 * 1024


def _params(*semantics):
    return pltpu.CompilerParams(dimension_semantics=semantics,
                                vmem_limit_bytes=VMEM_LIMIT_BYTES)


def _sigmoid(x):
    return 1.0 / (1.0 + jnp.exp(-x))


def _silu(x):
    return x * _sigmoid(x)


def _gelu_tanh(x):
    c = 0.7978845608028654
    return 0.5 * x * (1.0 + jnp.tanh(c * (x + 0.044715 * (x * x * x))))


def _rms(x, g):
    return x * lax.rsqrt(jnp.mean(x * x, axis=-1, keepdims=True) + EPS) * g


def _prenorm_kernel(x_ref, g_ref, h_ref):
    h_ref[...] = _rms(x_ref[...], g_ref[...]).astype(h_ref.dtype)


def prenorm(x, g, *, tr=256):
    t, d = x.shape
    return pl.pallas_call(
        _prenorm_kernel,
        grid=(t // tr,),
        in_specs=[pl.BlockSpec((tr, d), lambda i: (i, 0)),
                  pl.BlockSpec((1, d), lambda i: (0, 0))],
        out_specs=pl.BlockSpec((tr, d), lambda i: (i, 0)),
        out_shape=jax.ShapeDtypeStruct((t, d), BF16),
        compiler_params=_params("parallel"),
    )(x, g.reshape(1, d))


def _postnorm_kernel(y_ref, x_ref, gpost_ref, *rest, with_next):
    xn = x_ref[...] + _rms(y_ref[...].astype(F32), gpost_ref[...])
    if with_next:
        gpre_ref, xo_ref, h_ref = rest
        h_ref[...] = _rms(xn, gpre_ref[...]).astype(h_ref.dtype)
    else:
        (xo_ref,) = rest
    xo_ref[...] = xn


def postnorm_residual(y, x, g_post, g_pre_next=None, *, tr=256):
    t, d = x.shape
    with_next = g_pre_next is not None
    row = pl.BlockSpec((tr, d), lambda i: (i, 0))
    vec = pl.BlockSpec((1, d), lambda i: (0, 0))
    in_specs = [row, row, vec]
    args = [y, x, g_post.reshape(1, d)]
    out_specs = [row]
    out_shape = [jax.ShapeDtypeStruct((t, d), F32)]
    if with_next:
        in_specs.append(vec)
        args.append(g_pre_next.reshape(1, d))
        out_specs.append(row)
        out_shape.append(jax.ShapeDtypeStruct((t, d), BF16))
    out = pl.pallas_call(
        functools.partial(_postnorm_kernel, with_next=with_next),
        grid=(t // tr,),
        in_specs=in_specs,
        out_specs=out_specs,
        out_shape=out_shape,
        compiler_params=_params("parallel"),
    )(*args)
    return (out[0], out[1]) if with_next else (out[0], None)


def _mm_stream_kernel(a_ref, w_ref, o_ref, *, act, w_transposed):
    w = w_ref[...].astype(BF16)
    contract_w = 1 if w_transposed else 0
    r = lax.dot_general(a_ref[...], w, (((1,), (contract_w,)), ((), ())),
                        preferred_element_type=F32)
    if act == "gelu":
        r = _gelu_tanh(r)
    o_ref[...] = r.astype(o_ref.dtype)


def matmul_wstream(a, w, layer, *, tm, tn, n_cols=None, out_dtype=F32, act=None,
                   w_transposed=False):
    m, kdim = a.shape
    n = w.shape[1 if w_transposed else 2] if n_cols is None else n_cols
    assert m % tm == 0 and n % tn == 0
    if w_transposed:
        wspec = pl.BlockSpec((None, tn, kdim), lambda i, j: (layer, j, 0))
    else:
        wspec = pl.BlockSpec((None, kdim, tn), lambda i, j: (layer, 0, j))
    return pl.pallas_call(
        functools.partial(_mm_stream_kernel, act=act, w_transposed=w_transposed),
        grid=(m // tm, n // tn),
        in_specs=[pl.BlockSpec((tm, kdim), lambda i, j: (i, 0)), wspec],
        out_specs=pl.BlockSpec((tm, tn), lambda i, j: (i, j)),
        out_shape=jax.ShapeDtypeStruct((m, n), out_dtype),
        compiler_params=_params("parallel", "parallel"),
    )(a, w)


def _gdn_gates_kernel(a_ref, w_ref, alog_ref, dtb_ref, o_ref, wb_ref, *, n_heads):
    @pl.when(pl.program_id(0) == 0)
    def _():
        wb_ref[...] = w_ref[...].astype(BF16)

    logits = lax.dot_general(wb_ref[...], a_ref[...], (((1,), (1,)), ((), ())),
                             preferred_element_type=F32)
    o_ref[:n_heads, :] = _sigmoid(logits[:n_heads, :])
    dl = logits[n_heads:2 * n_heads, :] + dtb_ref[...]
    softplus = jnp.maximum(dl, 0.0) + jnp.log1p(jnp.exp(-jnp.abs(dl)))
    s = -jnp.exp(alog_ref[...]) * softplus
    pos = lax.broadcasted_iota(jnp.int32, s.shape, 1) & (GDN_BLOCK - 1)
    shift = 1
    while shift < GDN_BLOCK:
        s = s + jnp.where(pos >= shift, pltpu.roll(s, shift, axis=1), 0.0)
        shift *= 2
    o_ref[n_heads:, :] = s


def gdn_gates(a, w_t, layer, row0, a_log, dt_bias, *, tn):
    n_heads = a_log.shape[0]
    m, kdim = a.shape
    assert row0 % LANES == 0 and 2 * n_heads <= LANES
    col = pl.BlockSpec((n_heads, 1), lambda i: (0, 0))
    return pl.pallas_call(
        functools.partial(_gdn_gates_kernel, n_heads=n_heads),
        grid=(m // tn,),
        in_specs=[pl.BlockSpec((tn, kdim), lambda i: (i, 0)),
                  pl.BlockSpec((None, LANES, kdim), lambda i: (layer, row0 // LANES, 0)),
                  col, col],
        out_specs=pl.BlockSpec((2 * n_heads, tn), lambda i: (0, i)),
        out_shape=jax.ShapeDtypeStruct((2 * n_heads, m), F32),
        scratch_shapes=[pltpu.VMEM((LANES, kdim), BF16)],
        compiler_params=_params("arbitrary"),
    )(a, w_t, a_log.astype(F32).reshape(n_heads, 1), dt_bias.astype(F32).reshape(n_heads, 1))


def _ffn_up_kernel(a_ref, wg_ref, wu_ref, wd_ref, o_ref, wd_bf_ref):
    @pl.when(pl.program_id(0) == 0)
    def _():
        wd_bf_ref[...] = wd_ref[...].astype(BF16)

    a = a_ref[...]
    g = jnp.dot(a, wg_ref[...].astype(BF16), preferred_element_type=F32)
    u = jnp.dot(a, wu_ref[...].astype(BF16), preferred_element_type=F32)
    o_ref[...] = (_silu(g) * u).astype(o_ref.dtype)


def ffn_up(a, wg, wu, wd, layer, *, tm, tn):
    m, kdim = a.shape
    n = wg.shape[2]
    d_out = wd.shape[2]
    assert m % tm == 0 and n % tn == 0
    nb = n // tn
    wspec = pl.BlockSpec((None, kdim, tn), lambda i, j: (layer, 0, j))
    wd_rows = lambda i, j: jnp.where(i == 0, j, nb - 1)
    return pl.pallas_call(
        _ffn_up_kernel,
        grid=(m // tm, nb),
        in_specs=[pl.BlockSpec((tm, kdim), lambda i, j: (i, 0), pipeline_mode=pl.Buffered(1)),
                  wspec, wspec,
                  pl.BlockSpec((None, tn, d_out), lambda i, j: (layer, wd_rows(i, j), 0))],
        out_specs=[pl.BlockSpec((tm, tn), lambda i, j: (i, j)),
                   pl.BlockSpec((tn, d_out), lambda i, j: (wd_rows(i, j), 0))],
        out_shape=[jax.ShapeDtypeStruct((m, n), BF16), jax.ShapeDtypeStruct((n, d_out), BF16)],
        compiler_params=_params("arbitrary", "arbitrary"),
    )(a, wg, wu, wd)


def _pool_kernel(halo_ref, x_ref, w_ref, scale_ref, o_ref, *, tb, seq, group):
    start = (pl.program_id(0) * tb) % seq
    keep_halo = (start > 0).astype(F32)
    pos = start + 1 + lax.broadcasted_iota(jnp.int32, (tb, 1), 0)
    for gi, win in enumerate(POOL_WINDOWS):
        cols = slice(gi * group, (gi + 1) * group)
        x = x_ref[:, cols].astype(F32)
        s = jnp.concatenate([halo_ref[:, cols].astype(F32) * keep_halo, x], axis=0)
        shift = 1
        while shift < win:
            s = s + pltpu.roll(s, shift, axis=0)
            shift *= 2
        cnt = jnp.minimum(pos, win).astype(F32)
        y = s[POOL_HALO:, :] / cnt - x
        r = jnp.dot(y.astype(BF16), w_ref[gi], preferred_element_type=F32)
        o_ref[:, cols] = (r * scale_ref[:, cols]).astype(o_ref.dtype)
    width = len(POOL_WINDOWS) * group
    o_ref[:, width:] = jnp.zeros((tb, o_ref.shape[1] - width), o_ref.dtype)


def pool_mixer(z, w_grp, scale, *, seq, out_width, tb=256):
    t = z.shape[0]
    n_g, group, _ = w_grp.shape
    width = n_g * group
    hb = tb // POOL_HALO
    return pl.pallas_call(
        functools.partial(_pool_kernel, tb=tb, seq=seq, group=group),
        grid=(t // tb,),
        in_specs=[pl.BlockSpec((POOL_HALO, width), lambda i: (jnp.maximum(i * hb - 1, 0), 0)),
                  pl.BlockSpec((tb, width), lambda i: (i, 0)),
                  pl.BlockSpec((n_g, group, group), lambda i: (0, 0, 0)),
                  pl.BlockSpec((1, width), lambda i: (0, 0))],
        out_specs=pl.BlockSpec((tb, out_width), lambda i: (i, 0)),
        out_shape=jax.ShapeDtypeStruct((t, out_width), BF16),
        compiler_params=_params("parallel"),
    )(z, z, w_grp, scale.reshape(1, width))


def _gdn_kernel(q_ref, k_ref, v_ref, gate_ref, gates_ref, cq_ref, ck_ref, cv_ref, nw_ref,
                y_hbm_ref, o_ref, halo_ref, state_ref, *, hps, n_heads):
    del y_hbm_ref
    c = GDN_BLOCK
    d = GDN_HEAD_DIM
    halo = SUBLANES
    heads = range(hps)
    t = pl.program_id(2)
    h0 = pl.program_id(1) * hps

    @pl.when(t == 0)
    def _():
        halo_ref[...] = jnp.zeros_like(halo_ref)
        state_ref[...] = jnp.zeros_like(state_ref)

    width = hps * d
    sub = lax.broadcasted_iota(jnp.int32, (1, halo, width), 1)

    def conv_silu(x_ref, idx, cw_ref):
        cur = x_ref[...].astype(F32)
        tiles = jnp.concatenate([halo_ref[idx], cur], axis=0).reshape(c // halo + 1, halo, width)
        w = cw_ref[...]
        acc = cur * w[CONV_WIDTH - 1:CONV_WIDTH, :]
        for back in range(1, CONV_WIDTH):
            rot = pltpu.roll(tiles, back, axis=1)
            tap = jnp.where(sub < back, rot[:-1], rot[1:]).reshape(c, width)
            acc = acc + tap * w[CONV_WIDTH - 1 - back:CONV_WIDTH - back, :]
        halo_ref[idx] = cur[c - halo:, :]
        return _silu(acc)

    qs = conv_silu(q_ref, 0, cq_ref)
    ks = conv_silu(k_ref, 1, ck_ref)
    vs = conv_silu(v_ref, 2, cv_ref)

    ii = lax.broadcasted_iota(jnp.int32, (c, c), 0)
    jj = lax.broadcasted_iota(jnp.int32, (c, c), 1)
    causal = ii >= jj
    strict = ii > jj

    def bf(a):
        return a.astype(BF16)

    def mm(a, b):
        return jnp.dot(a, b, preferred_element_type=F32)

    def mm_nt(a, b):
        return lax.dot_general(a, b, (((1,), (1,)), ((), ())), preferred_element_type=F32)

    def l2norm(x):
        return x * lax.rsqrt(jnp.sum(x * x, axis=-1, keepdims=True) + EPS)

    q = [l2norm(qs[:, j * d:(j + 1) * d]) * (d ** -0.5) for j in heads]
    k = [l2norm(ks[:, j * d:(j + 1) * d]) for j in heads]
    v = [vs[:, j * d:(j + 1) * d] for j in heads]
    k_bf = [bf(x) for x in k]

    beta_rows = gates_ref[pl.ds(h0, hps), :]
    gc_rows = gates_ref[pl.ds(n_heads + h0, hps), :]
    beta = [jnp.broadcast_to(beta_rows[j:j + 1, :], (c, c)).T for j in heads]
    gc_t = [jnp.broadcast_to(gc_rows[j:j + 1, :], (c, c)) for j in heads]
    gc = [g.T for g in gc_t]
    decay = [jnp.exp(jnp.where(causal, gc[j] - gc_t[j], -jnp.inf)) for j in heads]
    gamma = [jnp.exp(g) for g in gc]
    gc_last = [g[c - 1:c, :] for g in gc]
    kb = [k[j] * beta[j] for j in heads]
    lmat = [jnp.where(strict, mm_nt(bf(kb[j]), k_bf[j]) * decay[j], 0.0) for j in heads]
    attn = [mm_nt(bf(q[j]), k_bf[j]) * decay[j] for j in heads]

    n = [-m for m in lmat]
    p_bf = [bf(m) for m in lmat]
    span = 2
    while span < c:
        p = [mm(m, m) for m in p_bf]
        p_bf = [bf(m) for m in p]
        n = [n[j] + p[j] + mm(bf(n[j]), p_bf[j]) for j in heads]
        span *= 2
    rhs = [jnp.concatenate([kb[j] * gamma[j], v[j] * beta[j]], axis=1) for j in heads]
    sol = [rhs[j] + mm(bf(n[j]), bf(rhs[j])) for j in heads]
    qd = [q[j] * gamma[j] for j in heads]
    kd_t = [(k[j] * jnp.exp(gc_last[j] - gc[j])).T for j in heads]

    state = [state_ref[j] for j in heads]
    state_bf = [bf(m) for m in state]
    v_new = [sol[j][:, d:] - mm(bf(sol[j][:, :d]), state_bf[j]) for j in heads]
    v_new_bf = [bf(m) for m in v_new]
    o = [mm(bf(qd[j]), state_bf[j]) + mm(bf(attn[j]), v_new_bf[j]) for j in heads]
    for j in heads:
        state_ref[j] = jnp.exp(gc_last[j]) * state[j] + mm(bf(kd_t[j]), v_new_bf[j])
    for j in heads:
        gate = gate_ref[:, j * d:(j + 1) * d].astype(F32)
        o_ref[:, j * d:(j + 1) * d] = (_rms(o[j], nw_ref[...]) * _silu(gate)).astype(o_ref.dtype)


def gated_deltanet(z, gates, conv_w, norm_w, y_mix, *, batch, seq, col0, width):
    d = GDN_HEAD_DIM
    c = GDN_BLOCK
    hps = GDN_HEADS_PER_STEP
    n_heads = width // d
    bw = hps * d
    nblk = seq // c
    cb0 = col0 // bw
    wb = width // bw
    ocb0 = (y_mix.shape[1] - width) // bw

    def zspec(part):
        return pl.BlockSpec((c, bw), lambda b, h, t: (b * nblk + t, cb0 + part * wb + h))

    def cspec(part):
        return pl.BlockSpec((CONV_WIDTH, bw), lambda b, h, t: (0, part * wb + h))

    return pl.pallas_call(
        functools.partial(_gdn_kernel, hps=hps, n_heads=n_heads),
        grid=(batch, n_heads // hps, nblk),
        in_specs=[zspec(0), zspec(1), zspec(2), zspec(3),
                  pl.BlockSpec((2 * n_heads, c), lambda b, h, t: (0, b * nblk + t)),
                  cspec(0), cspec(1), cspec(2),
                  pl.BlockSpec((1, d), lambda b, h, t: (0, 0)),
                  pl.BlockSpec(memory_space=pl.ANY)],
        out_specs=pl.BlockSpec((c, bw), lambda b, h, t: (b * nblk + t, ocb0 + h)),
        out_shape=jax.ShapeDtypeStruct(y_mix.shape, y_mix.dtype),
        input_output_aliases={9: 0},
        scratch_shapes=[pltpu.VMEM((3, SUBLANES, bw), F32), pltpu.VMEM((hps, d, d), F32)],
        compiler_params=_params("parallel", "parallel", "arbitrary"),
    )(z, z, z, z, gates, conv_w, conv_w, conv_w, norm_w.reshape(1, d), y_mix)


def _sgu_kernel(u_ref, v_ref, lng_ref, lnb_ref, ws_ref, bs_ref, o_ref, *, group):
    v = v_ref[...].astype(F32)
    mu = jnp.mean(v, axis=-1, keepdims=True)
    xc = v - mu
    vn = xc * lax.rsqrt(jnp.mean(xc * xc, axis=-1, keepdims=True) + EPS) * lng_ref[...] + lnb_ref[...]
    vb = vn.astype(BF16)
    ii = lax.broadcasted_iota(jnp.int32, (SGU_LEN, SGU_LEN), 0)
    jj = lax.broadcasted_iota(jnp.int32, (SGU_LEN, SGU_LEN), 1)
    chunk_bits = SGU_CHUNK.bit_length() - 1
    mask = (jj >> chunk_bits) <= (ii >> chunk_bits)
    for g in range(SGU_GROUPS):
        cols = slice(g * group, (g + 1) * group)
        ws = jnp.where(mask, ws_ref[g], 0.0).astype(BF16)
        sv = jnp.dot(ws, vb[:, cols], preferred_element_type=F32) + bs_ref[:, g:g + 1]
        o_ref[:, cols] = (u_ref[:, cols].astype(F32) * sv).astype(o_ref.dtype)


def spatial_gating(z, ln_g, ln_b, w_s, b_s):
    t, w2 = z.shape
    width = w2 // 2
    group = width // SGU_GROUPS
    vec = pl.BlockSpec((1, width), lambda i: (0, 0))
    return pl.pallas_call(
        functools.partial(_sgu_kernel, group=group),
        grid=(t // SGU_LEN,),
        in_specs=[pl.BlockSpec((SGU_LEN, width), lambda i: (i, 0)),
                  pl.BlockSpec((SGU_LEN, width), lambda i: (i, 1)),
                  vec, vec,
                  pl.BlockSpec((SGU_GROUPS, SGU_LEN, SGU_LEN), lambda i: (0, 0, 0)),
                  pl.BlockSpec((SGU_LEN, SGU_GROUPS), lambda i: (0, 0))],
        out_specs=pl.BlockSpec((SGU_LEN, width), lambda i: (i, 0)),
        out_shape=jax.ShapeDtypeStruct((t, width), BF16),
        compiler_params=_params("parallel"),
    )(z, z, ln_g.reshape(1, width), ln_b.reshape(1, width), w_s, b_s.T)


def _mix_pool_gdn(h, i, w_in, pool_w, pool_scale, conv_w, a_log, dt_bias, norm_w, w_out, *, batch, seq):
    n_g, group, _ = pool_w.shape
    pool_width = n_g * group
    n_heads = a_log.shape[0]
    gdn_width = n_heads * GDN_HEAD_DIM
    main = pool_width + 4 * gdn_width
    w_in_t = jnp.swapaxes(w_in, 1, 2)
    z = matmul_wstream(h, w_in_t, i, n_cols=main, tm=1024, tn=1024, out_dtype=BF16,
                       w_transposed=True)
    gates = gdn_gates(h, w_in_t, i, main, a_log, dt_bias, tn=1024)
    y = pool_mixer(z, pool_w.astype(BF16), pool_scale, seq=seq, out_width=pool_width + gdn_width)
    y = gated_deltanet(z, gates, conv_w, norm_w, y, batch=batch, seq=seq, col0=pool_width,
                       width=gdn_width)
    return matmul_wstream(y, w_out, i, tm=1024, tn=1024, out_dtype=BF16)


def _mix_sgu(h, i, w_in, ln_g, ln_b, w_s, b_s, w_out):
    z = matmul_wstream(h, w_in, i, tm=1024, tn=1024, out_dtype=BF16, act="gelu")
    s = spatial_gating(z, ln_g, ln_b, w_s, b_s)
    return matmul_wstream(s, w_out, i, tm=1024, tn=1024, out_dtype=BF16)


def _swiglu(h, layer, w_gate, w_up, w_down):
    act, wd = ffn_up(h, w_gate, w_up, w_down, layer, tm=2048, tn=256)
    return matmul_wstream(act, wd[None], 0, tm=512, tn=512, out_dtype=BF16)


def kernel(x, norm_mix_pre, norm_mix_post, norm_ffn_pre, norm_ffn_post, ab_w_in, pool_w, pool_scale, gdn_conv, gdn_a_log, gdn_dt_bias, gdn_norm, ab_w_out, sgu_w_in, sgu_ln_g, sgu_ln_b, sgu_w_s, sgu_b_s, sgu_w_out, ffn_w_gate, ffn_w_up, ffn_w_down):
    batch, seq, d_model = x.shape
    depth = norm_mix_pre.shape[0]
    xt = x.reshape(batch * seq, d_model)
    h = prenorm(xt, norm_mix_pre[0])
    for layer in range(depth):
        i = layer // 2
        if layer % 2 == 0:
            y = _mix_pool_gdn(h, i, ab_w_in, pool_w[i], pool_scale[i], gdn_conv[i], gdn_a_log[i],
                              gdn_dt_bias[i], gdn_norm[i], ab_w_out, batch=batch, seq=seq)
        else:
            y = _mix_sgu(h, i, sgu_w_in, sgu_ln_g[i], sgu_ln_b[i], sgu_w_s[i], sgu_b_s[i], sgu_w_out)
        xt, h = postnorm_residual(y, xt, norm_mix_post[layer], norm_ffn_pre[layer])
        y = _swiglu(h, layer, ffn_w_gate, ffn_w_up, ffn_w_down)
        nxt = norm_mix_pre[layer + 1] if layer + 1 < depth else None
        xt, h = postnorm_residual(y, xt, norm_ffn_post[layer], nxt)
    return xt.reshape(batch, seq, d_model)
```

```python
import functools

import jax
import jax.numpy as jnp
from jax import lax
from jax.experimental import pallas as pl
from jax.experimental.pallas import tpu as pltpu

F32 = jnp.float32
BF16 = jnp.bfloat16

EPS = 1e-6
POOL_WINDOWS = (2, 4, 8, 16)
POOL_HALO = 16
GDN_HEAD_DIM = 128
CONV_WIDTH = 4
GDN_BLOCK = 128
GDN_HEADS_PER_STEP = 16
SGU_GROUPS = 16
SGU_LEN = 128
SGU_CHUNK = 64
LANES = 128
SUBLANES = 8
VMEM_LIMIT_BYTES = 62 * 1024 * 1024


def _params(*semantics):
    return pltpu.CompilerParams(dimension_semantics=semantics,
                                vmem_limit_bytes=VMEM_LIMIT_BYTES)


def _sigmoid(x):
    return 0.5 + 0.5 * jnp.tanh(0.5 * x)


def _silu(x):
    h = 0.5 * x
    return h + h * jnp.tanh(h)


def _gelu_tanh(x):
    c = 0.7978845608028654
    h = 0.5 * x
    return h + h * jnp.tanh(x * (c + (c * 0.044715) * (x * x)))


def _rms(x, g):
    return x * lax.rsqrt(jnp.mean(x * x, axis=-1, keepdims=True) + EPS) * g


def _prenorm_kernel(x_ref, g_ref, h_ref):
    h_ref[...] = _rms(x_ref[...], g_ref[...]).astype(h_ref.dtype)


def prenorm(x, g, *, tr=256):
    t, d = x.shape
    return pl.pallas_call(
        _prenorm_kernel,
        grid=(t // tr,),
        in_specs=[pl.BlockSpec((tr, d), lambda i: (i, 0)),
                  pl.BlockSpec((1, d), lambda i: (0, 0))],
        out_specs=pl.BlockSpec((tr, d), lambda i: (i, 0)),
        out_shape=jax.ShapeDtypeStruct((t, d), BF16),
        compiler_params=_params("parallel"),
    )(x, g.reshape(1, d))


def _postnorm_kernel(y_ref, x_ref, gpost_ref, *rest, with_next):
    xn = x_ref[...] + _rms(y_ref[...].astype(F32), gpost_ref[...])
    if with_next:
        gpre_ref, xo_ref, h_ref = rest
        h_ref[...] = _rms(xn, gpre_ref[...]).astype(h_ref.dtype)
    else:
        (xo_ref,) = rest
    xo_ref[...] = xn


def postnorm_residual(y, x, g_post, g_pre_next=None, *, tr=256):
    t, d = x.shape
    with_next = g_pre_next is not None
    row = pl.BlockSpec((tr, d), lambda i: (i, 0))
    vec = pl.BlockSpec((1, d), lambda i: (0, 0))
    in_specs = [row, row, vec]
    args = [y, x, g_post.reshape(1, d)]
    out_specs = [row]
    out_shape = [jax.ShapeDtypeStruct((t, d), F32)]
    if with_next:
        in_specs.append(vec)
        args.append(g_pre_next.reshape(1, d))
        out_specs.append(row)
        out_shape.append(jax.ShapeDtypeStruct((t, d), BF16))
    out = pl.pallas_call(
        functools.partial(_postnorm_kernel, with_next=with_next),
        grid=(t // tr,),
        in_specs=in_specs,
        out_specs=out_specs,
        out_shape=out_shape,
        compiler_params=_params("parallel"),
    )(*args)
    return (out[0], out[1]) if with_next else (out[0], None)


def _mm_stream_kernel(a_ref, w_ref, o_ref, *, act, w_transposed):
    w = w_ref[...].astype(BF16)
    contract_w = 1 if w_transposed else 0
    r = lax.dot_general(a_ref[...], w, (((1,), (contract_w,)), ((), ())),
                        preferred_element_type=F32)
    if act == "gelu":
        r = _gelu_tanh(r)
    o_ref[...] = r.astype(o_ref.dtype)


def matmul_wstream(a, w, layer, *, tm, tn, n_cols=None, out_dtype=F32, act=None,
                   w_transposed=False):
    m, kdim = a.shape
    n = w.shape[1 if w_transposed else 2] if n_cols is None else n_cols
    assert m % tm == 0 and n % tn == 0
    if w_transposed:
        wspec = pl.BlockSpec((None, tn, kdim), lambda i, j: (layer, j, 0))
    else:
        wspec = pl.BlockSpec((None, kdim, tn), lambda i, j: (layer, 0, j))
    return pl.pallas_call(
        functools.partial(_mm_stream_kernel, act=act, w_transposed=w_transposed),
        grid=(m // tm, n // tn),
        in_specs=[pl.BlockSpec((tm, kdim), lambda i, j: (i, 0)), wspec],
        out_specs=pl.BlockSpec((tm, tn), lambda i, j: (i, j)),
        out_shape=jax.ShapeDtypeStruct((m, n), out_dtype),
        compiler_params=_params("parallel", "parallel"),
    )(a, w)


def _gdn_gates_kernel(a_ref, w_ref, alog_ref, dtb_ref, o_ref, wb_ref, *, n_heads):
    @pl.when(pl.program_id(0) == 0)
    def _():
        wb_ref[...] = w_ref[...].astype(BF16)

    logits = lax.dot_general(wb_ref[...], a_ref[...], (((1,), (1,)), ((), ())),
                             preferred_element_type=F32)
    o_ref[:n_heads, :] = _sigmoid(logits[:n_heads, :])
    dl = logits[n_heads:2 * n_heads, :] + dtb_ref[...]
    softplus = jnp.maximum(dl, 0.0) + jnp.log1p(jnp.exp(-jnp.abs(dl)))
    s = -jnp.exp(alog_ref[...]) * softplus
    pos = lax.broadcasted_iota(jnp.int32, s.shape, 1) & (GDN_BLOCK - 1)
    shift = 1
    while shift < GDN_BLOCK:
        s = s + jnp.where(pos >= shift, pltpu.roll(s, shift, axis=1), 0.0)
        shift *= 2
    o_ref[n_heads:, :] = s


def gdn_gates(a, w_t, layer, row0, a_log, dt_bias, *, tn):
    n_heads = a_log.shape[0]
    m, kdim = a.shape
    assert row0 % LANES == 0 and 2 * n_heads <= LANES
    col = pl.BlockSpec((n_heads, 1), lambda i: (0, 0))
    return pl.pallas_call(
        functools.partial(_gdn_gates_kernel, n_heads=n_heads),
        grid=(m // tn,),
        in_specs=[pl.BlockSpec((tn, kdim), lambda i: (i, 0)),
                  pl.BlockSpec((None, LANES, kdim), lambda i: (layer, row0 // LANES, 0)),
                  col, col],
        out_specs=pl.BlockSpec((2 * n_heads, tn), lambda i: (0, i)),
        out_shape=jax.ShapeDtypeStruct((2 * n_heads, m), F32),
        scratch_shapes=[pltpu.VMEM((LANES, kdim), BF16)],
        compiler_params=_params("arbitrary"),
    )(a, w_t, a_log.astype(F32).reshape(n_heads, 1), dt_bias.astype(F32).reshape(n_heads, 1))


def _ffn_up_kernel(a_ref, wg_ref, wu_ref, wd_ref, o_ref, wd_bf_ref):
    @pl.when(pl.program_id(0) == 0)
    def _():
        wd_bf_ref[...] = wd_ref[...].astype(BF16)

    a = a_ref[...]
    g = jnp.dot(a, wg_ref[...].astype(BF16), preferred_element_type=F32)
    u = jnp.dot(a, wu_ref[...].astype(BF16), preferred_element_type=F32)
    o_ref[...] = (_silu(g) * u).astype(o_ref.dtype)


def ffn_up(a, wg, wu, wd, layer, *, tm, tn):
    m, kdim = a.shape
    n = wg.shape[2]
    d_out = wd.shape[2]
    assert m % tm == 0 and n % tn == 0
    nb = n // tn
    wspec = pl.BlockSpec((None, kdim, tn), lambda i, j: (layer, 0, j))
    wd_rows = lambda i, j: jnp.where(i == 0, j, nb - 1)
    return pl.pallas_call(
        _ffn_up_kernel,
        grid=(m // tm, nb),
        in_specs=[pl.BlockSpec((tm, kdim), lambda i, j: (i, 0), pipeline_mode=pl.Buffered(1)),
                  wspec, wspec,
                  pl.BlockSpec((None, tn, d_out), lambda i, j: (layer, wd_rows(i, j), 0))],
        out_specs=[pl.BlockSpec((tm, tn), lambda i, j: (i, j)),
                   pl.BlockSpec((tn, d_out), lambda i, j: (wd_rows(i, j), 0))],
        out_shape=[jax.ShapeDtypeStruct((m, n), BF16), jax.ShapeDtypeStruct((n, d_out), BF16)],
        compiler_params=_params("arbitrary", "arbitrary"),
    )(a, wg, wu, wd)


def _pool_kernel(halo_ref, x_ref, w_ref, scale_ref, o_ref, *, tb, seq, group):
    start = (pl.program_id(0) * tb) % seq
    keep_halo = (start > 0).astype(F32)
    pos = start + 1 + lax.broadcasted_iota(jnp.int32, (tb, 1), 0)
    for gi, win in enumerate(POOL_WINDOWS):
        cols = slice(gi * group, (gi + 1) * group)
        x = x_ref[:, cols].astype(F32)
        s = jnp.concatenate([halo_ref[:, cols].astype(F32) * keep_halo, x], axis=0)
        shift = 1
        while shift < win:
            s = s + pltpu.roll(s, shift, axis=0)
            shift *= 2
        cnt = jnp.minimum(pos, win).astype(F32)
        y = s[POOL_HALO:, :] / cnt - x
        r = jnp.dot(y.astype(BF16), w_ref[gi], preferred_element_type=F32)
        o_ref[:, cols] = (r * scale_ref[:, cols]).astype(o_ref.dtype)
    width = len(POOL_WINDOWS) * group
    o_ref[:, width:] = jnp.zeros((tb, o_ref.shape[1] - width), o_ref.dtype)


def pool_mixer(z, w_grp, scale, *, seq, out_width, tb=256):
    t = z.shape[0]
    n_g, group, _ = w_grp.shape
    width = n_g * group
    hb = tb // POOL_HALO
    return pl.pallas_call(
        functools.partial(_pool_kernel, tb=tb, seq=seq, group=group),
        grid=(t // tb,),
        in_specs=[pl.BlockSpec((POOL_HALO, width), lambda i: (jnp.maximum(i * hb - 1, 0), 0)),
                  pl.BlockSpec((tb, width), lambda i: (i, 0)),
                  pl.BlockSpec((n_g, group, group), lambda i: (0, 0, 0)),
                  pl.BlockSpec((1, width), lambda i: (0, 0))],
        out_specs=pl.BlockSpec((tb, out_width), lambda i: (i, 0)),
        out_shape=jax.ShapeDtypeStruct((t, out_width), BF16),
        compiler_params=_params("parallel"),
    )(z, z, w_grp, scale.reshape(1, width))


def _gdn_kernel(q_ref, k_ref, v_ref, gate_ref, gates_ref, cq_ref, ck_ref, cv_ref, nw_ref,
                y_hbm_ref, o_ref, halo_ref, state_ref, *, hps, n_heads):
    del y_hbm_ref
    c = GDN_BLOCK
    d = GDN_HEAD_DIM
    halo = SUBLANES
    heads = range(hps)
    t = pl.program_id(2)
    h0 = pl.program_id(1) * hps

    @pl.when(t == 0)
    def _():
        halo_ref[...] = jnp.zeros_like(halo_ref)
        state_ref[...] = jnp.zeros_like(state_ref)

    width = hps * d
    sub = lax.broadcasted_iota(jnp.int32, (1, halo, width), 1)

    def conv_silu(x_ref, idx, cw_ref):
        cur = x_ref[...].astype(F32)
        tiles = jnp.concatenate([halo_ref[idx], cur], axis=0).reshape(c // halo + 1, halo, width)
        w = cw_ref[...]
        acc = cur * w[CONV_WIDTH - 1:CONV_WIDTH, :]
        for back in range(1, CONV_WIDTH):
            rot = pltpu.roll(tiles, back, axis=1)
            tap = jnp.where(sub < back, rot[:-1], rot[1:]).reshape(c, width)
            acc = acc + tap * w[CONV_WIDTH - 1 - back:CONV_WIDTH - back, :]
        halo_ref[idx] = cur[c - halo:, :]
        return _silu(acc)

    qs = conv_silu(q_ref, 0, cq_ref)
    ks = conv_silu(k_ref, 1, ck_ref)
    vs = conv_silu(v_ref, 2, cv_ref)

    ii = lax.broadcasted_iota(jnp.int32, (c, c), 0)
    jj = lax.broadcasted_iota(jnp.int32, (c, c), 1)
    causal = ii >= jj
    strict = ii > jj

    def bf(a):
        return a.astype(BF16)

    def mm(a, b):
        return jnp.dot(a, b, preferred_element_type=F32)

    def mm_nt(a, b):
        return lax.dot_general(a, b, (((1,), (1,)), ((), ())), preferred_element_type=F32)

    def l2norm(x):
        return x * lax.rsqrt(jnp.sum(x * x, axis=-1, keepdims=True) + EPS)

    q = [l2norm(qs[:, j * d:(j + 1) * d]) * (d ** -0.5) for j in heads]
    k = [l2norm(ks[:, j * d:(j + 1) * d]) for j in heads]
    v = [vs[:, j * d:(j + 1) * d] for j in heads]
    k_bf = [bf(x) for x in k]

    beta_rows = gates_ref[pl.ds(h0, hps), :]
    gc_rows = gates_ref[pl.ds(n_heads + h0, hps), :]
    beta = [jnp.broadcast_to(beta_rows[j:j + 1, :], (c, c)).T for j in heads]
    gc_t = [jnp.broadcast_to(gc_rows[j:j + 1, :], (c, c)) for j in heads]
    gc = [g.T for g in gc_t]
    decay = [jnp.exp(jnp.where(causal, gc[j] - gc_t[j], -jnp.inf)) for j in heads]
    gamma = [jnp.exp(g) for g in gc]
    gc_last = [g[c - 1:c, :] for g in gc]
    kb = [k[j] * beta[j] for j in heads]
    lmat = [jnp.where(strict, mm_nt(bf(kb[j]), k_bf[j]) * decay[j], 0.0) for j in heads]
    attn = [mm_nt(bf(q[j]), k_bf[j]) * decay[j] for j in heads]

    n = [-m for m in lmat]
    p_bf = [bf(m) for m in lmat]
    span = 2
    while span < c:
        p = [mm(m, m) for m in p_bf]
        p_bf = [bf(m) for m in p]
        n = [n[j] + p[j] + mm(bf(n[j]), p_bf[j]) for j in heads]
        span *= 2
    rhs = [jnp.concatenate([kb[j] * gamma[j], v[j] * beta[j]], axis=1) for j in heads]
    sol = [rhs[j] + mm(bf(n[j]), bf(rhs[j])) for j in heads]
    qd = [q[j] * gamma[j] for j in heads]
    kd_t = [(k[j] * jnp.exp(gc_last[j] - gc[j])).T for j in heads]

    state = [state_ref[j] for j in heads]
    state_bf = [bf(m) for m in state]
    v_new = [sol[j][:, d:] - mm(bf(sol[j][:, :d]), state_bf[j]) for j in heads]
    v_new_bf = [bf(m) for m in v_new]
    o = [mm(bf(qd[j]), state_bf[j]) + mm(bf(attn[j]), v_new_bf[j]) for j in heads]
    for j in heads:
        state_ref[j] = jnp.exp(gc_last[j]) * state[j] + mm(bf(kd_t[j]), v_new_bf[j])
    for j in heads:
        gate = gate_ref[:, j * d:(j + 1) * d].astype(F32)
        o_ref[:, j * d:(j + 1) * d] = (_rms(o[j], nw_ref[...]) * _silu(gate)).astype(o_ref.dtype)


def gated_deltanet(z, gates, conv_w, norm_w, y_mix, *, batch, seq, col0, width):
    d = GDN_HEAD_DIM
    c = GDN_BLOCK
    hps = GDN_HEADS_PER_STEP
    n_heads = width // d
    bw = hps * d
    nblk = seq // c
    cb0 = col0 // bw
    wb = width // bw
    ocb0 = (y_mix.shape[1] - width) // bw

    def zspec(part):
        return pl.BlockSpec((c, bw), lambda b, h, t: (b * nblk + t, cb0 + part * wb + h))

    def cspec(part):
        return pl.BlockSpec((CONV_WIDTH, bw), lambda b, h, t: (0, part * wb + h))

    return pl.pallas_call(
        functools.partial(_gdn_kernel, hps=hps, n_heads=n_heads),
        grid=(batch, n_heads // hps, nblk),
        in_specs=[zspec(0), zspec(1), zspec(2), zspec(3),
                  pl.BlockSpec((2 * n_heads, c), lambda b, h, t: (0, b * nblk + t)),
                  cspec(0), cspec(1), cspec(2),
                  pl.BlockSpec((1, d), lambda b, h, t: (0, 0)),
                  pl.BlockSpec(memory_space=pl.ANY)],
        out_specs=pl.BlockSpec((c, bw), lambda b, h, t: (b * nblk + t, ocb0 + h)),
        out_shape=jax.ShapeDtypeStruct(y_mix.shape, y_mix.dtype),
        input_output_aliases={9: 0},
        scratch_shapes=[pltpu.VMEM((3, SUBLANES, bw), F32), pltpu.VMEM((hps, d, d), F32)],
        compiler_params=_params("parallel", "parallel", "arbitrary"),
    )(z, z, z, z, gates, conv_w, conv_w, conv_w, norm_w.reshape(1, d), y_mix)


def _sgu_kernel(u_ref, v_ref, lng_ref, lnb_ref, ws_ref, bs_ref, o_ref, *, group):
    v = v_ref[...].astype(F32)
    mu = jnp.mean(v, axis=-1, keepdims=True)
    xc = v - mu
    vn = xc * lax.rsqrt(jnp.mean(xc * xc, axis=-1, keepdims=True) + EPS) * lng_ref[...] + lnb_ref[...]
    vb = vn.astype(BF16)
    ii = lax.broadcasted_iota(jnp.int32, (SGU_LEN, SGU_LEN), 0)
    jj = lax.broadcasted_iota(jnp.int32, (SGU_LEN, SGU_LEN), 1)
    chunk_bits = SGU_CHUNK.bit_length() - 1
    mask = (jj >> chunk_bits) <= (ii >> chunk_bits)
    for g in range(SGU_GROUPS):
        cols = slice(g * group, (g + 1) * group)
        ws = jnp.where(mask, ws_ref[g], 0.0).astype(BF16)
        sv = jnp.dot(ws, vb[:, cols], preferred_element_type=F32) + bs_ref[:, g:g + 1]
        o_ref[:, cols] = (u_ref[:, cols].astype(F32) * sv).astype(o_ref.dtype)


def spatial_gating(z, ln_g, ln_b, w_s, b_s):
    t, w2 = z.shape
    width = w2 // 2
    group = width // SGU_GROUPS
    vec = pl.BlockSpec((1, width), lambda i: (0, 0))
    return pl.pallas_call(
        functools.partial(_sgu_kernel, group=group),
        grid=(t // SGU_LEN,),
        in_specs=[pl.BlockSpec((SGU_LEN, width), lambda i: (i, 0)),
                  pl.BlockSpec((SGU_LEN, width), lambda i: (i, 1)),
                  vec, vec,
                  pl.BlockSpec((SGU_GROUPS, SGU_LEN, SGU_LEN), lambda i: (0, 0, 0)),
                  pl.BlockSpec((SGU_LEN, SGU_GROUPS), lambda i: (0, 0))],
        out_specs=pl.BlockSpec((SGU_LEN, width), lambda i: (i, 0)),
        out_shape=jax.ShapeDtypeStruct((t, width), BF16),
        compiler_params=_params("parallel"),
    )(z, z, ln_g.reshape(1, width), ln_b.reshape(1, width), w_s, b_s.T)


def _mix_pool_gdn(h, i, w_in, pool_w, pool_scale, conv_w, a_log, dt_bias, norm_w, w_out, *, batch, seq):
    n_g, group, _ = pool_w.shape
    pool_width = n_g * group
    n_heads = a_log.shape[0]
    gdn_width = n_heads * GDN_HEAD_DIM
    main = pool_width + 4 * gdn_width
    w_in_t = jnp.swapaxes(w_in, 1, 2)
    z = matmul_wstream(h, w_in_t, i, n_cols=main, tm=1024, tn=1024, out_dtype=BF16,
                       w_transposed=True)
    gates = gdn_gates(h, w_in_t, i, main, a_log, dt_bias, tn=1024)
    y = pool_mixer(z, pool_w.astype(BF16), pool_scale, seq=seq, out_width=pool_width + gdn_width)
    y = gated_deltanet(z, gates, conv_w, norm_w, y, batch=batch, seq=seq, col0=pool_width,
                       width=gdn_width)
    return matmul_wstream(y, w_out, i, tm=1024, tn=1024, out_dtype=BF16)


def _mix_sgu(h, i, w_in, ln_g, ln_b, w_s, b_s, w_out):
    z = matmul_wstream(h, w_in, i, tm=1024, tn=1024, out_dtype=BF16, act="gelu")
    s = spatial_gating(z, ln_g, ln_b, w_s, b_s)
    return matmul_wstream(s, w_out, i, tm=1024, tn=1024, out_dtype=BF16)


def _swiglu(h, layer, w_gate, w_up, w_down):
    act, wd = ffn_up(h, w_gate, w_up, w_down, layer, tm=2048, tn=256)
    return matmul_wstream(act, wd[None], 0, tm=512, tn=512, out_dtype=BF16)


def kernel(x, norm_mix_pre, norm_mix_post, norm_ffn_pre, norm_ffn_post, ab_w_in, pool_w, pool_scale, gdn_conv, gdn_a_log, gdn_dt_bias, gdn_norm, ab_w_out, sgu_w_in, sgu_ln_g, sgu_ln_b, sgu_w_s, sgu_b_s, sgu_w_out, ffn_w_gate, ffn_w_up, ffn_w_down):
    batch, seq, d_model = x.shape
    depth = norm_mix_pre.shape[0]
    xt = x.reshape(batch * seq, d_model)
    h = prenorm(xt, norm_mix_pre[0])
    for layer in range(depth):
        i = layer // 2
        if layer % 2 == 0:
            y = _mix_pool_gdn(h, i, ab_w_in, pool_w[i], pool_scale[i], gdn_conv[i], gdn_a_log[i],
                              gdn_dt_bias[i], gdn_norm[i], ab_w_out, batch=batch, seq=seq)
        else:
            y = _mix_sgu(h, i, sgu_w_in, sgu_ln_g[i], sgu_ln_b[i], sgu_w_s[i], sgu_b_s[i], sgu_w_out)
        xt, h = postnorm_residual(y, xt, norm_mix_post[layer], norm_ffn_pre[layer])
        y = _swiglu(h, layer, ffn_w_gate, ffn_w_up, ffn_w_down)
        nxt = norm_mix_pre[layer + 1] if layer + 1 < depth else None
        xt, h = postnorm_residual(y, xt, norm_ffn_post[layer], nxt)
    return xt.reshape(batch, seq, d_model)
```

```python
import functools

import jax
import jax.numpy as jnp
from jax import lax
from jax.experimental import pallas as pl
from jax.experimental.pallas import tpu as pltpu

F32 = jnp.float32
BF16 = jnp.bfloat16

EPS = 1e-6
POOL_WINDOWS = (2, 4, 8, 16)
POOL_HALO = 16
GDN_HEAD_DIM = 128
CONV_WIDTH = 4
GDN_BLOCK = 128
GDN_HEADS_PER_STEP = 16
SGU_GROUPS = 16
SGU_LEN = 128
SGU_CHUNK = 64
LANES = 128
BF16_ROWS = 16
VMEM_LIMIT_BYTES = 62 * 1024 * 1024


def _params(*semantics):
    return pltpu.CompilerParams(dimension_semantics=semantics,
                                vmem_limit_bytes=VMEM_LIMIT_BYTES)


def _sigmoid(x):
    return 0.5 + 0.5 * jnp.tanh(0.5 * x)


def _silu(x):
    h = 0.5 * x
    return h + h * jnp.tanh(h)


def _gelu_tanh(x):
    c = 0.7978845608028654
    h = 0.5 * x
    return h + h * jnp.tanh(x * (c + (c * 0.044715) * (x * x)))


def _rms(x, g):
    return x * lax.rsqrt(jnp.mean(x * x, axis=-1, keepdims=True) + EPS) * g


def _prenorm_kernel(x_ref, g_ref, h_ref):
    h_ref[...] = _rms(x_ref[...], g_ref[...]).astype(h_ref.dtype)


def prenorm(x, g, *, tr=512):
    t, d = x.shape
    return pl.pallas_call(
        _prenorm_kernel,
        grid=(t // tr,),
        in_specs=[pl.BlockSpec((tr, d), lambda i: (i, 0)),
                  pl.BlockSpec((1, d), lambda i: (0, 0))],
        out_specs=pl.BlockSpec((tr, d), lambda i: (i, 0)),
        out_shape=jax.ShapeDtypeStruct((t, d), BF16),
        compiler_params=_params("parallel"),
    )(x, g.reshape(1, d))


def _postnorm_kernel(y_ref, x_ref, gpost_ref, *rest, with_next):
    xn = x_ref[...] + _rms(y_ref[...].astype(F32), gpost_ref[...])
    if with_next:
        gpre_ref, xo_ref, h_ref = rest
        h_ref[...] = _rms(xn, gpre_ref[...]).astype(h_ref.dtype)
    else:
        (xo_ref,) = rest
    xo_ref[...] = xn


def postnorm_residual(y, x, g_post, g_pre_next=None, *, tr=512):
    t, d = x.shape
    with_next = g_pre_next is not None
    row = pl.BlockSpec((tr, d), lambda i: (i, 0))
    vec = pl.BlockSpec((1, d), lambda i: (0, 0))
    in_specs = [row, row, vec]
    args = [y, x, g_post.reshape(1, d)]
    out_specs = [row]
    out_shape = [jax.ShapeDtypeStruct((t, d), F32)]
    if with_next:
        in_specs.append(vec)
        args.append(g_pre_next.reshape(1, d))
        out_specs.append(row)
        out_shape.append(jax.ShapeDtypeStruct((t, d), BF16))
    out = pl.pallas_call(
        functools.partial(_postnorm_kernel, with_next=with_next),
        grid=(t // tr,),
        in_specs=in_specs,
        out_specs=out_specs,
        out_shape=out_shape,
        compiler_params=_params("parallel"),
    )(*args)
    return (out[0], out[1]) if with_next else (out[0], None)


def _mm_stream_kernel(a_ref, w_ref, o_ref, *, act, w_transposed):
    w = w_ref[...].astype(BF16)
    contract_w = 1 if w_transposed else 0
    r = lax.dot_general(a_ref[...], w, (((1,), (contract_w,)), ((), ())),
                        preferred_element_type=F32)
    if act == "gelu":
        r = _gelu_tanh(r)
    o_ref[...] = r.astype(o_ref.dtype)


def matmul_wstream(a, w, layer, *, tm, tn, n_cols=None, out_dtype=F32, act=None,
                   w_transposed=False):
    m, kdim = a.shape
    if layer is None:
        n = w.shape[0] * w.shape[2]
        assert w.shape[2] == tn
        wspec = pl.BlockSpec((None, kdim, tn), lambda i, j: (j, 0, 0))
    elif w_transposed:
        n = w.shape[1] if n_cols is None else n_cols
        wspec = pl.BlockSpec((None, tn, kdim), lambda i, j: (layer, j, 0))
    else:
        n = w.shape[2] if n_cols is None else n_cols
        wspec = pl.BlockSpec((None, kdim, tn), lambda i, j: (layer, 0, j))
    assert m % tm == 0 and n % tn == 0
    return pl.pallas_call(
        functools.partial(_mm_stream_kernel, act=act, w_transposed=w_transposed),
        grid=(m // tm, n // tn),
        in_specs=[pl.BlockSpec((tm, kdim), lambda i, j: (i, 0)), wspec],
        out_specs=pl.BlockSpec((tm, tn), lambda i, j: (i, j)),
        out_shape=jax.ShapeDtypeStruct((m, n), out_dtype),
        compiler_params=_params("parallel", "parallel"),
    )(a, w)


def _gdn_gates_kernel(a_ref, w_ref, alog_ref, dtb_ref, o_ref, wb_ref, *, n_heads):
    @pl.when(pl.program_id(0) == 0)
    def _():
        wb_ref[...] = w_ref[...].astype(BF16)

    logits = lax.dot_general(wb_ref[...], a_ref[...], (((1,), (1,)), ((), ())),
                             preferred_element_type=F32)
    o_ref[:n_heads, :] = _sigmoid(logits[:n_heads, :])
    dl = logits[n_heads:2 * n_heads, :] + dtb_ref[...]
    softplus = jnp.maximum(dl, 0.0) + jnp.log1p(jnp.exp(-jnp.abs(dl)))
    s = -jnp.exp(alog_ref[...]) * softplus
    pos = lax.broadcasted_iota(jnp.int32, s.shape, 1) & (GDN_BLOCK - 1)
    shift = 1
    while shift < GDN_BLOCK:
        s = s + jnp.where(pos >= shift, pltpu.roll(s, shift, axis=1), 0.0)
        shift *= 2
    o_ref[n_heads:, :] = s


def gdn_gates(a, w_t, layer, row0, a_log, dt_bias, *, tn):
    n_heads = a_log.shape[0]
    m, kdim = a.shape
    assert row0 % LANES == 0 and 2 * n_heads <= LANES
    col = pl.BlockSpec((n_heads, 1), lambda i: (0, 0))
    return pl.pallas_call(
        functools.partial(_gdn_gates_kernel, n_heads=n_heads),
        grid=(m // tn,),
        in_specs=[pl.BlockSpec((tn, kdim), lambda i: (i, 0)),
                  pl.BlockSpec((None, LANES, kdim), lambda i: (layer, row0 // LANES, 0)),
                  col, col],
        out_specs=pl.BlockSpec((2 * n_heads, tn), lambda i: (0, i)),
        out_shape=jax.ShapeDtypeStruct((2 * n_heads, m), F32),
        scratch_shapes=[pltpu.VMEM((LANES, kdim), BF16)],
        compiler_params=_params("arbitrary"),
    )(a, w_t, a_log.astype(F32).reshape(n_heads, 1), dt_bias.astype(F32).reshape(n_heads, 1))


def _ffn_up_kernel(a_ref, wg_ref, wu_ref, wd_ref, o_ref, wd_bf_ref):
    @pl.when(pl.program_id(0) == 0)
    def _():
        slab = wd_bf_ref.shape[2]
        for p in range(wd_bf_ref.shape[0]):
            wd_bf_ref[p] = wd_ref[:, p * slab:(p + 1) * slab].astype(BF16)

    a = a_ref[...]
    g = jnp.dot(a, wg_ref[...].astype(BF16), preferred_element_type=F32)
    u = jnp.dot(a, wu_ref[...].astype(BF16), preferred_element_type=F32)
    o_ref[...] = (_silu(g) * u).astype(o_ref.dtype)


def ffn_up(a, wg, wu, wd, layer, *, tm, tn, wd_slab):
    m, kdim = a.shape
    n = wg.shape[2]
    d_out = wd.shape[2]
    assert m % tm == 0 and n % tn == 0 and d_out % wd_slab == 0
    nb = n // tn
    n_slabs = d_out // wd_slab
    wspec = pl.BlockSpec((None, kdim, tn), lambda i, j: (layer, 0, j))
    wd_rows = lambda i, j: jnp.where(i == 0, j, nb - 1)
    return pl.pallas_call(
        _ffn_up_kernel,
        grid=(m // tm, nb),
        in_specs=[pl.BlockSpec((tm, kdim), lambda i, j: (i, 0), pipeline_mode=pl.Buffered(1)),
                  wspec, wspec,
                  pl.BlockSpec((None, tn, d_out), lambda i, j: (layer, wd_rows(i, j), 0))],
        out_specs=[pl.BlockSpec((tm, tn), lambda i, j: (i, j)),
                   pl.BlockSpec((n_slabs, tn, wd_slab), lambda i, j: (0, wd_rows(i, j), 0))],
        out_shape=[jax.ShapeDtypeStruct((m, n), BF16),
                   jax.ShapeDtypeStruct((n_slabs, n, wd_slab), BF16)],
        compiler_params=_params("arbitrary", "arbitrary"),
    )(a, wg, wu, wd)


def _pool_kernel(halo_ref, x_ref, w_ref, scale_ref, o_ref, *, tb, seq, group):
    start = (pl.program_id(0) * tb) % seq
    keep_halo = (start > 0).astype(F32)
    pos = start + 1 + lax.broadcasted_iota(jnp.int32, (tb, 1), 0)
    for gi, win in enumerate(POOL_WINDOWS):
        cols = slice(gi * group, (gi + 1) * group)
        x = x_ref[:, cols].astype(F32)
        s = jnp.concatenate([halo_ref[:, cols].astype(F32) * keep_halo, x], axis=0)
        shift = 1
        while shift < win:
            s = s + pltpu.roll(s, shift, axis=0)
            shift *= 2
        cnt = jnp.minimum(pos, win).astype(F32)
        y = s[POOL_HALO:, :] / cnt - x
        r = jnp.dot(y.astype(BF16), w_ref[gi], preferred_element_type=F32)
        o_ref[:, cols] = (r * scale_ref[:, cols]).astype(o_ref.dtype)
    width = len(POOL_WINDOWS) * group
    o_ref[:, width:] = jnp.zeros((tb, o_ref.shape[1] - width), o_ref.dtype)


def pool_mixer(z, w_grp, scale, *, seq, out_width, tb=256):
    t = z.shape[0]
    n_g, group, _ = w_grp.shape
    width = n_g * group
    hb = tb // POOL_HALO
    return pl.pallas_call(
        functools.partial(_pool_kernel, tb=tb, seq=seq, group=group),
        grid=(t // tb,),
        in_specs=[pl.BlockSpec((POOL_HALO, width), lambda i: (jnp.maximum(i * hb - 1, 0), 0)),
                  pl.BlockSpec((tb, width), lambda i: (i, 0)),
                  pl.BlockSpec((n_g, group, group), lambda i: (0, 0, 0)),
                  pl.BlockSpec((1, width), lambda i: (0, 0))],
        out_specs=pl.BlockSpec((tb, out_width), lambda i: (i, 0)),
        out_shape=jax.ShapeDtypeStruct((t, out_width), BF16),
        compiler_params=_params("parallel"),
    )(z, z, w_grp, scale.reshape(1, width))


def _gdn_kernel(q_ref, k_ref, v_ref, gate_ref, gates_ref, cq_ref, ck_ref, cv_ref, nw_ref,
                y_hbm_ref, o_ref, halo_ref, state_ref, *, hps, n_heads):
    del y_hbm_ref
    c = GDN_BLOCK
    d = GDN_HEAD_DIM
    halo = BF16_ROWS
    heads = range(hps)
    t = pl.program_id(2)
    h0 = pl.program_id(1) * hps

    @pl.when(t == 0)
    def _():
        halo_ref[...] = jnp.zeros_like(halo_ref)
        state_ref[...] = jnp.zeros_like(state_ref)

    n_back = CONV_WIDTH - 1
    pick_row = lax.broadcasted_iota(jnp.int32, (n_back * c, halo + c), 0)
    pick_col = lax.broadcasted_iota(jnp.int32, (n_back * c, halo + c), 1)
    wanted = (pick_row & (c - 1)) + halo - 1 - (pick_row >> (c.bit_length() - 1))
    shift_rows = jnp.where(pick_col == wanted, 1.0, 0.0).astype(BF16)

    def conv_silu(x_ref, idx, cw_ref):
        cur = x_ref[...]
        taps = jnp.dot(shift_rows, jnp.concatenate([halo_ref[idx], cur], axis=0),
                       preferred_element_type=F32)
        w = cw_ref[...]
        acc = cur.astype(F32) * w[CONV_WIDTH - 1:CONV_WIDTH, :]
        for back in range(1, CONV_WIDTH):
            acc = acc + taps[(back - 1) * c:back * c, :] * w[CONV_WIDTH - 1 - back:CONV_WIDTH - back, :]
        halo_ref[idx] = cur[c - halo:, :]
        return _silu(acc)

    qs = conv_silu(q_ref, 0, cq_ref)
    ks = conv_silu(k_ref, 1, ck_ref)
    vs = conv_silu(v_ref, 2, cv_ref)

    ii = lax.broadcasted_iota(jnp.int32, (c, c), 0)
    jj = lax.broadcasted_iota(jnp.int32, (c, c), 1)
    causal = ii >= jj
    strict = ii > jj

    def bf(a):
        return a.astype(BF16)

    def mm(a, b):
        return jnp.dot(a, b, preferred_element_type=F32)

    def mm_nt(a, b):
        return lax.dot_general(a, b, (((1,), (1,)), ((), ())), preferred_element_type=F32)

    def l2norm(x):
        return x * lax.rsqrt(jnp.sum(x * x, axis=-1, keepdims=True) + EPS)

    beta_rows = gates_ref[pl.ds(h0, hps), :]
    gc_rows = gates_ref[pl.ds(n_heads + h0, hps), :]

    decay, gc_last, kk, qk, rhs, qd_bf, kd_t_bf = [], [], [], [], [], [], []
    for j in heads:
        q = l2norm(qs[:, j * d:(j + 1) * d]) * (d ** -0.5)
        k = l2norm(ks[:, j * d:(j + 1) * d])
        v = vs[:, j * d:(j + 1) * d]
        beta = jnp.broadcast_to(beta_rows[j:j + 1, :], (c, c)).T
        gc_t = jnp.broadcast_to(gc_rows[j:j + 1, :], (c, c))
        gc = gc_t.T
        gamma = jnp.exp(gc)
        last = gc[c - 1:c, :]
        kb = k * beta
        k_bf = bf(k)
        kk.append(mm_nt(bf(kb), k_bf))
        qk.append(mm_nt(bf(q), k_bf))
        decay.append(jnp.exp(jnp.where(causal, gc - gc_t, -jnp.inf)))
        gc_last.append(last)
        rhs.append(jnp.concatenate([kb * gamma, v * beta], axis=1))
        qd_bf.append(bf(q * gamma))
        kd_t_bf.append(bf((k * jnp.exp(last - gc)).T))
    lmat = [jnp.where(strict, kk[j] * decay[j], 0.0) for j in heads]
    attn_bf = [bf(qk[j] * decay[j]) for j in heads]

    n = [-m for m in lmat]
    p_bf = [bf(m) for m in lmat]
    span = 2
    while span < c:
        p = [mm(m, m) for m in p_bf]
        p_bf = [bf(m) for m in p]
        n = [n[j] + p[j] + mm(bf(n[j]), p_bf[j]) for j in heads]
        span *= 2
    sol = [rhs[j] + mm(bf(n[j]), bf(rhs[j])) for j in heads]

    state = [state_ref[j] for j in heads]
    state_bf = [bf(m) for m in state]
    v_new = [sol[j][:, d:] - mm(bf(sol[j][:, :d]), state_bf[j]) for j in heads]
    v_new_bf = [bf(m) for m in v_new]
    o = [mm(qd_bf[j], state_bf[j]) + mm(attn_bf[j], v_new_bf[j]) for j in heads]
    for j in heads:
        state_ref[j] = jnp.exp(gc_last[j]) * state[j] + mm(kd_t_bf[j], v_new_bf[j])
    for j in heads:
        gate = gate_ref[:, j * d:(j + 1) * d].astype(F32)
        o_ref[:, j * d:(j + 1) * d] = (_rms(o[j], nw_ref[...]) * _silu(gate)).astype(o_ref.dtype)


def gated_deltanet(z, gates, conv_w, norm_w, y_mix, *, batch, seq, col0, width):
    d = GDN_HEAD_DIM
    c = GDN_BLOCK
    hps = GDN_HEADS_PER_STEP
    n_heads = width // d
    assert z.dtype == BF16
    bw = hps * d
    nblk = seq // c
    cb0 = col0 // bw
    wb = width // bw
    ocb0 = (y_mix.shape[1] - width) // bw

    def zspec(part):
        return pl.BlockSpec((c, bw), lambda b, h, t: (b * nblk + t, cb0 + part * wb + h))

    def cspec(part):
        return pl.BlockSpec((CONV_WIDTH, bw), lambda b, h, t: (0, part * wb + h))

    return pl.pallas_call(
        functools.partial(_gdn_kernel, hps=hps, n_heads=n_heads),
        grid=(batch, n_heads // hps, nblk),
        in_specs=[zspec(0), zspec(1), zspec(2), zspec(3),
                  pl.BlockSpec((2 * n_heads, c), lambda b, h, t: (0, b * nblk + t)),
                  cspec(0), cspec(1), cspec(2),
                  pl.BlockSpec((1, d), lambda b, h, t: (0, 0)),
                  pl.BlockSpec(memory_space=pl.ANY)],
        out_specs=pl.BlockSpec((c, bw), lambda b, h, t: (b * nblk + t, ocb0 + h)),
        out_shape=jax.ShapeDtypeStruct(y_mix.shape, y_mix.dtype),
        input_output_aliases={9: 0},
        scratch_shapes=[pltpu.VMEM((3, BF16_ROWS, bw), BF16), pltpu.VMEM((hps, d, d), F32)],
        compiler_params=_params("parallel", "parallel", "arbitrary"),
    )(z, z, z, z, gates, conv_w, conv_w, conv_w, norm_w.reshape(1, d), y_mix)


def _sgu_kernel(u_ref, v_ref, lng_ref, lnb_ref, ws_ref, bs_ref, o_ref, *, group):
    v = v_ref[...].astype(F32)
    mu = jnp.mean(v, axis=-1, keepdims=True)
    xc = v - mu
    vn = xc * lax.rsqrt(jnp.mean(xc * xc, axis=-1, keepdims=True) + EPS) * lng_ref[...] + lnb_ref[...]
    vb = vn.astype(BF16)
    ii = lax.broadcasted_iota(jnp.int32, (SGU_LEN, SGU_LEN), 0)
    jj = lax.broadcasted_iota(jnp.int32, (SGU_LEN, SGU_LEN), 1)
    chunk_bits = SGU_CHUNK.bit_length() - 1
    mask = (jj >> chunk_bits) <= (ii >> chunk_bits)
    for g in range(SGU_GROUPS):
        cols = slice(g * group, (g + 1) * group)
        ws = jnp.where(mask, ws_ref[g], 0.0).astype(BF16)
        sv = jnp.dot(ws, vb[:, cols], preferred_element_type=F32) + bs_ref[:, g:g + 1]
        o_ref[:, cols] = (u_ref[:, cols].astype(F32) * sv).astype(o_ref.dtype)


def spatial_gating(z, ln_g, ln_b, w_s, b_s):
    t, w2 = z.shape
    width = w2 // 2
    group = width // SGU_GROUPS
    vec = pl.BlockSpec((1, width), lambda i: (0, 0))
    return pl.pallas_call(
        functools.partial(_sgu_kernel, group=group),
        grid=(t // SGU_LEN,),
        in_specs=[pl.BlockSpec((SGU_LEN, width), lambda i: (i, 0)),
                  pl.BlockSpec((SGU_LEN, width), lambda i: (i, 1)),
                  vec, vec,
                  pl.BlockSpec((SGU_GROUPS, SGU_LEN, SGU_LEN), lambda i: (0, 0, 0)),
                  pl.BlockSpec((SGU_LEN, SGU_GROUPS), lambda i: (0, 0))],
        out_specs=pl.BlockSpec((SGU_LEN, width), lambda i: (i, 0)),
        out_shape=jax.ShapeDtypeStruct((t, width), BF16),
        compiler_params=_params("parallel"),
    )(z, z, ln_g.reshape(1, width), ln_b.reshape(1, width), w_s, b_s.T)


def _mix_pool_gdn(h, i, w_in, pool_w, pool_scale, conv_w, a_log, dt_bias, norm_w, w_out, *, batch, seq):
    n_g, group, _ = pool_w.shape
    pool_width = n_g * group
    n_heads = a_log.shape[0]
    gdn_width = n_heads * GDN_HEAD_DIM
    main = pool_width + 4 * gdn_width
    w_in_t = jnp.swapaxes(w_in, 1, 2)
    z = matmul_wstream(h, w_in_t, i, n_cols=main, tm=1024, tn=1024, out_dtype=BF16,
                       w_transposed=True)
    gates = gdn_gates(h, w_in_t, i, main, a_log, dt_bias, tn=1024)
    y = pool_mixer(z, pool_w.astype(BF16), pool_scale, seq=seq, out_width=pool_width + gdn_width)
    y = gated_deltanet(z, gates, conv_w, norm_w, y, batch=batch, seq=seq, col0=pool_width,
                       width=gdn_width)
    return matmul_wstream(y, w_out, i, tm=1024, tn=1024, out_dtype=BF16)


def _mix_sgu(h, i, w_in, ln_g, ln_b, w_s, b_s, w_out):
    z = matmul_wstream(h, w_in, i, tm=1024, tn=1024, out_dtype=BF16, act="gelu")
    s = spatial_gating(z, ln_g, ln_b, w_s, b_s)
    return matmul_wstream(s, w_out, i, tm=1024, tn=1024, out_dtype=BF16)


def _swiglu(h, layer, w_gate, w_up, w_down):
    act, wd = ffn_up(h, w_gate, w_up, w_down, layer, tm=2048, tn=256, wd_slab=512)
    return matmul_wstream(act, wd, None, tm=512, tn=512, out_dtype=BF16)


def kernel(x, norm_mix_pre, norm_mix_post, norm_ffn_pre, norm_ffn_post, ab_w_in, pool_w, pool_scale, gdn_conv, gdn_a_log, gdn_dt_bias, gdn_norm, ab_w_out, sgu_w_in, sgu_ln_g, sgu_ln_b, sgu_w_s, sgu_b_s, sgu_w_out, ffn_w_gate, ffn_w_up, ffn_w_down):
    batch, seq, d_model = x.shape
    depth = norm_mix_pre.shape[0]
    xt = x.reshape(batch * seq, d_model)
    h = prenorm(xt, norm_mix_pre[0])
    for layer in range(depth):
        i = layer // 2
        if layer % 2 == 0:
            y = _mix_pool_gdn(h, i, ab_w_in, pool_w[i], pool_scale[i], gdn_conv[i], gdn_a_log[i],
                              gdn_dt_bias[i], gdn_norm[i], ab_w_out, batch=batch, seq=seq)
        else:
            y = _mix_sgu(h, i, sgu_w_in, sgu_ln_g[i], sgu_ln_b[i], sgu_w_s[i], sgu_b_s[i], sgu_w_out)
        xt, h = postnorm_residual(y, xt, norm_mix_post[layer], norm_ffn_pre[layer])
        y = _swiglu(h, layer, ffn_w_gate, ffn_w_up, ffn_w_down)
        nxt = norm_mix_pre[layer + 1] if layer + 1 < depth else None
        xt, h = postnorm_residual(y, xt, norm_ffn_post[layer], nxt)
    return xt.reshape(batch, seq, d_model)
```

```python
import functools

import jax
import jax.numpy as jnp
from jax import lax
from jax.experimental import pallas as pl
from jax.experimental.pallas import tpu as pltpu

F32 = jnp.float32
BF16 = jnp.bfloat16

EPS = 1e-6
POOL_WINDOWS = (2, 4, 8, 16)
POOL_HALO = 16
GDN_HEAD_DIM = 128
CONV_WIDTH = 4
GDN_BLOCK = 128
GDN_HEADS_PER_STEP = 16
SGU_GROUPS = 16
SGU_LEN = 128
SGU_CHUNK = 64
SGU_BLOCKS_PER_STEP = 2
LANES = 128
SUBLANES = 8
VMEM_LIMIT_BYTES = 62 * 1024 * 1024


def _params(*semantics):
    return pltpu.CompilerParams(dimension_semantics=semantics,
                                vmem_limit_bytes=VMEM_LIMIT_BYTES)


def _sigmoid(x):
    return 0.5 + 0.5 * jnp.tanh(0.5 * x)


def _silu(x):
    h = 0.5 * x
    return h + h * jnp.tanh(h)


def _gelu_tanh(x):
    c = 0.7978845608028654
    h = 0.5 * x
    return h + h * jnp.tanh(x * (c + (c * 0.044715) * (x * x)))


def _rms(x, g):
    return x * lax.rsqrt(jnp.mean(x * x, axis=-1, keepdims=True) + EPS) * g


def _prenorm_kernel(x_ref, g_ref, h_ref):
    h_ref[...] = _rms(x_ref[...], g_ref[...]).astype(h_ref.dtype)


def prenorm(x, g, *, tr=512):
    t, d = x.shape
    return pl.pallas_call(
        _prenorm_kernel,
        grid=(t // tr,),
        in_specs=[pl.BlockSpec((tr, d), lambda i: (i, 0)),
                  pl.BlockSpec((1, d), lambda i: (0, 0))],
        out_specs=pl.BlockSpec((tr, d), lambda i: (i, 0)),
        out_shape=jax.ShapeDtypeStruct((t, d), BF16),
        compiler_params=_params("parallel"),
    )(x, g.reshape(1, d))


def _postnorm_kernel(y_ref, x_ref, gpost_ref, *rest, with_next):
    xn = x_ref[...] + _rms(y_ref[...].astype(F32), gpost_ref[...])
    if with_next:
        gpre_ref, xo_ref, h_ref = rest
        h_ref[...] = _rms(xn, gpre_ref[...]).astype(h_ref.dtype)
    else:
        (xo_ref,) = rest
    xo_ref[...] = xn


def postnorm_residual(y, x, g_post, g_pre_next=None, *, tr=512):
    t, d = x.shape
    with_next = g_pre_next is not None
    row = pl.BlockSpec((tr, d), lambda i: (i, 0))
    vec = pl.BlockSpec((1, d), lambda i: (0, 0))
    in_specs = [row, row, vec]
    args = [y, x, g_post.reshape(1, d)]
    out_specs = [row]
    out_shape = [jax.ShapeDtypeStruct((t, d), F32)]
    if with_next:
        in_specs.append(vec)
        args.append(g_pre_next.reshape(1, d))
        out_specs.append(row)
        out_shape.append(jax.ShapeDtypeStruct((t, d), BF16))
    out = pl.pallas_call(
        functools.partial(_postnorm_kernel, with_next=with_next),
        grid=(t // tr,),
        in_specs=in_specs,
        out_specs=out_specs,
        out_shape=out_shape,
        compiler_params=_params("parallel"),
    )(*args)
    return (out[0], out[1]) if with_next else (out[0], None)


def _mm_stream_kernel(a_ref, w_ref, o_ref, *, act, w_transposed):
    w = w_ref[...].astype(BF16)
    contract_w = 1 if w_transposed else 0
    r = lax.dot_general(a_ref[...], w, (((1,), (contract_w,)), ((), ())),
                        preferred_element_type=F32)
    if act == "gelu":
        r = _gelu_tanh(r)
    o_ref[...] = r.astype(o_ref.dtype)


def matmul_wstream(a, w, layer, *, tm, tn, n_cols=None, out_dtype=F32, act=None,
                   w_transposed=False):
    m, kdim = a.shape
    if layer is None:
        n = w.shape[0] * w.shape[2]
        assert w.shape[2] == tn
        wspec = pl.BlockSpec((None, kdim, tn), lambda i, j: (j, 0, 0))
    elif w_transposed:
        n = w.shape[1] if n_cols is None else n_cols
        wspec = pl.BlockSpec((None, tn, kdim), lambda i, j: (layer, j, 0))
    else:
        n = w.shape[2] if n_cols is None else n_cols
        wspec = pl.BlockSpec((None, kdim, tn), lambda i, j: (layer, 0, j))
    assert m % tm == 0 and n % tn == 0
    return pl.pallas_call(
        functools.partial(_mm_stream_kernel, act=act, w_transposed=w_transposed),
        grid=(m // tm, n // tn),
        in_specs=[pl.BlockSpec((tm, kdim), lambda i, j: (i, 0)), wspec],
        out_specs=pl.BlockSpec((tm, tn), lambda i, j: (i, j)),
        out_shape=jax.ShapeDtypeStruct((m, n), out_dtype),
        compiler_params=_params("parallel", "parallel"),
    )(a, w)


def _gdn_gates_kernel(a_ref, w_ref, alog_ref, dtb_ref, o_ref, wb_ref, *, n_heads):
    @pl.when(pl.program_id(0) == 0)
    def _():
        wb_ref[...] = w_ref[...].astype(BF16)

    logits = lax.dot_general(wb_ref[...], a_ref[...], (((1,), (1,)), ((), ())),
                             preferred_element_type=F32)
    o_ref[:n_heads, :] = _sigmoid(logits[:n_heads, :])
    dl = logits[n_heads:2 * n_heads, :] + dtb_ref[...]
    softplus = jnp.maximum(dl, 0.0) + jnp.log1p(jnp.exp(-jnp.abs(dl)))
    s = -jnp.exp(alog_ref[...]) * softplus
    pos = lax.broadcasted_iota(jnp.int32, s.shape, 1) & (GDN_BLOCK - 1)
    shift = 1
    while shift < GDN_BLOCK:
        s = s + jnp.where(pos >= shift, pltpu.roll(s, shift, axis=1), 0.0)
        shift *= 2
    o_ref[n_heads:, :] = s


def gdn_gates(a, w_t, layer, row0, a_log, dt_bias, *, tn):
    n_heads = a_log.shape[0]
    m, kdim = a.shape
    assert row0 % LANES == 0 and 2 * n_heads <= LANES
    col = pl.BlockSpec((n_heads, 1), lambda i: (0, 0))
    return pl.pallas_call(
        functools.partial(_gdn_gates_kernel, n_heads=n_heads),
        grid=(m // tn,),
        in_specs=[pl.BlockSpec((tn, kdim), lambda i: (i, 0)),
                  pl.BlockSpec((None, LANES, kdim), lambda i: (layer, row0 // LANES, 0)),
                  col, col],
        out_specs=pl.BlockSpec((2 * n_heads, tn), lambda i: (0, i)),
        out_shape=jax.ShapeDtypeStruct((2 * n_heads, m), F32),
        scratch_shapes=[pltpu.VMEM((LANES, kdim), BF16)],
        compiler_params=_params("arbitrary"),
    )(a, w_t, a_log.astype(F32).reshape(n_heads, 1), dt_bias.astype(F32).reshape(n_heads, 1))


def _ffn_up_kernel(a_ref, wg_ref, wu_ref, wd_ref, o_ref, wd_bf_ref):
    @pl.when(pl.program_id(0) == 0)
    def _():
        slab = wd_bf_ref.shape[2]
        for p in range(wd_bf_ref.shape[0]):
            wd_bf_ref[p] = wd_ref[:, p * slab:(p + 1) * slab].astype(BF16)

    a = a_ref[...]
    g = jnp.dot(a, wg_ref[...].astype(BF16), preferred_element_type=F32)
    u = jnp.dot(a, wu_ref[...].astype(BF16), preferred_element_type=F32)
    o_ref[...] = (_silu(g) * u).astype(o_ref.dtype)


def ffn_up(a, wg, wu, wd, layer, *, tm, tn, wd_slab):
    m, kdim = a.shape
    n = wg.shape[2]
    d_out = wd.shape[2]
    assert m % tm == 0 and n % tn == 0 and d_out % wd_slab == 0
    nb = n // tn
    n_slabs = d_out // wd_slab
    wspec = pl.BlockSpec((None, kdim, tn), lambda i, j: (layer, 0, j))
    wd_rows = lambda i, j: jnp.where(i == 0, j, nb - 1)
    return pl.pallas_call(
        _ffn_up_kernel,
        grid=(m // tm, nb),
        in_specs=[pl.BlockSpec((tm, kdim), lambda i, j: (i, 0), pipeline_mode=pl.Buffered(1)),
                  wspec, wspec,
                  pl.BlockSpec((None, tn, d_out), lambda i, j: (layer, wd_rows(i, j), 0))],
        out_specs=[pl.BlockSpec((tm, tn), lambda i, j: (i, j)),
                   pl.BlockSpec((n_slabs, tn, wd_slab), lambda i, j: (0, wd_rows(i, j), 0))],
        out_shape=[jax.ShapeDtypeStruct((m, n), BF16),
                   jax.ShapeDtypeStruct((n_slabs, n, wd_slab), BF16)],
        compiler_params=_params("arbitrary", "arbitrary"),
    )(a, wg, wu, wd)


def _pool_kernel(halo_ref, x_ref, w_ref, scale_ref, o_ref, *, tb, seq, group):
    start = (pl.program_id(0) * tb) % seq
    keep_halo = (start > 0).astype(F32)
    pos = start + 1 + lax.broadcasted_iota(jnp.int32, (tb, 1), 0)
    for gi, win in enumerate(POOL_WINDOWS):
        cols = slice(gi * group, (gi + 1) * group)
        x = x_ref[:, cols].astype(F32)
        s = jnp.concatenate([halo_ref[:, cols].astype(F32) * keep_halo, x], axis=0)
        shift = 1
        while shift < win:
            s = s + pltpu.roll(s, shift, axis=0)
            shift *= 2
        cnt = jnp.minimum(pos, win).astype(F32)
        y = s[POOL_HALO:, :] / cnt - x
        r = jnp.dot(y.astype(BF16), w_ref[gi], preferred_element_type=F32)
        o_ref[:, cols] = (r * scale_ref[:, cols]).astype(o_ref.dtype)
    width = len(POOL_WINDOWS) * group
    o_ref[:, width:] = jnp.zeros((tb, o_ref.shape[1] - width), o_ref.dtype)


def pool_mixer(z, w_grp, scale, *, seq, out_width, tb=256):
    t = z.shape[0]
    n_g, group, _ = w_grp.shape
    width = n_g * group
    hb = tb // POOL_HALO
    return pl.pallas_call(
        functools.partial(_pool_kernel, tb=tb, seq=seq, group=group),
        grid=(t // tb,),
        in_specs=[pl.BlockSpec((POOL_HALO, width), lambda i: (jnp.maximum(i * hb - 1, 0), 0)),
                  pl.BlockSpec((tb, width), lambda i: (i, 0)),
                  pl.BlockSpec((n_g, group, group), lambda i: (0, 0, 0)),
                  pl.BlockSpec((1, width), lambda i: (0, 0))],
        out_specs=pl.BlockSpec((tb, out_width), lambda i: (i, 0)),
        out_shape=jax.ShapeDtypeStruct((t, out_width), BF16),
        compiler_params=_params("parallel"),
    )(z, z, w_grp, scale.reshape(1, width))


def _gdn_kernel(q_ref, k_ref, v_ref, gate_ref, gates_ref, cq_ref, ck_ref, cv_ref, nw_ref,
                y_hbm_ref, o_ref, halo_ref, state_ref, *, hps, n_heads):
    del y_hbm_ref
    c = GDN_BLOCK
    d = GDN_HEAD_DIM
    halo = SUBLANES
    heads = range(hps)
    t = pl.program_id(2)
    h0 = pl.program_id(1) * hps

    @pl.when(t == 0)
    def _():
        halo_ref[...] = jnp.zeros_like(halo_ref)
        state_ref[...] = jnp.zeros_like(state_ref)

    width = hps * d
    sub = lax.broadcasted_iota(jnp.int32, (1, halo, width), 1)

    def conv_silu(x_ref, idx, cw_ref):
        cur = x_ref[...].astype(F32)
        tiles = jnp.concatenate([halo_ref[idx], cur], axis=0).reshape(c // halo + 1, halo, width)
        w = cw_ref[...]
        acc = cur * w[CONV_WIDTH - 1:CONV_WIDTH, :]
        for back in range(1, CONV_WIDTH):
            rot = pltpu.roll(tiles, back, axis=1)
            tap = jnp.where(sub < back, rot[:-1], rot[1:]).reshape(c, width)
            acc = acc + tap * w[CONV_WIDTH - 1 - back:CONV_WIDTH - back, :]
        halo_ref[idx] = cur[c - halo:, :]
        return _silu(acc)

    qs = conv_silu(q_ref, 0, cq_ref)
    ks = conv_silu(k_ref, 1, ck_ref)
    vs = conv_silu(v_ref, 2, cv_ref)

    ii = lax.broadcasted_iota(jnp.int32, (c, c), 0)
    jj = lax.broadcasted_iota(jnp.int32, (c, c), 1)
    causal = ii >= jj
    strict = ii > jj

    def bf(a):
        return a.astype(BF16)

    def mm(a, b):
        return jnp.dot(a, b, preferred_element_type=F32)

    def mm_nt(a, b):
        return lax.dot_general(a, b, (((1,), (1,)), ((), ())), preferred_element_type=F32)

    def l2norm(x):
        return x * lax.rsqrt(jnp.sum(x * x, axis=-1, keepdims=True) + EPS)

    beta_rows = gates_ref[pl.ds(h0, hps), :]
    gc_rows = gates_ref[pl.ds(n_heads + h0, hps), :]

    decay, gc_last, kk, qk, rhs, qd_bf, kd_t_bf = [], [], [], [], [], [], []
    for j in heads:
        q = l2norm(qs[:, j * d:(j + 1) * d]) * (d ** -0.5)
        k = l2norm(ks[:, j * d:(j + 1) * d])
        v = vs[:, j * d:(j + 1) * d]
        beta = jnp.broadcast_to(beta_rows[j:j + 1, :], (c, c)).T
        gc_t = jnp.broadcast_to(gc_rows[j:j + 1, :], (c, c))
        gc = gc_t.T
        gamma = jnp.exp(gc)
        last = gc[c - 1:c, :]
        kb = k * beta
        k_bf = bf(k)
        kk.append(mm_nt(bf(kb), k_bf))
        qk.append(mm_nt(bf(q), k_bf))
        decay.append(jnp.exp(jnp.where(causal, gc - gc_t, -jnp.inf)))
        gc_last.append(last)
        rhs.append(jnp.concatenate([kb * gamma, v * beta], axis=1))
        qd_bf.append(bf(q * gamma))
        kd_t_bf.append(bf((k * jnp.exp(last - gc)).T))
    lmat = [jnp.where(strict, kk[j] * decay[j], 0.0) for j in heads]
    attn_bf = [bf(qk[j] * decay[j]) for j in heads]

    n = [-m for m in lmat]
    p_bf = [bf(m) for m in lmat]
    span = 2
    while span < c:
        p = [mm(m, m) for m in p_bf]
        p_bf = [bf(m) for m in p]
        n = [n[j] + p[j] + mm(bf(n[j]), p_bf[j]) for j in heads]
        span *= 2
    sol = [rhs[j] + mm(bf(n[j]), bf(rhs[j])) for j in heads]

    state = [state_ref[j] for j in heads]
    state_bf = [bf(m) for m in state]
    v_new = [sol[j][:, d:] - mm(bf(sol[j][:, :d]), state_bf[j]) for j in heads]
    v_new_bf = [bf(m) for m in v_new]
    o = [mm(qd_bf[j], state_bf[j]) + mm(attn_bf[j], v_new_bf[j]) for j in heads]
    for j in heads:
        state_ref[j] = jnp.exp(gc_last[j]) * state[j] + mm(kd_t_bf[j], v_new_bf[j])
    for j in heads:
        gate = gate_ref[:, j * d:(j + 1) * d].astype(F32)
        o_ref[:, j * d:(j + 1) * d] = (_rms(o[j], nw_ref[...]) * _silu(gate)).astype(o_ref.dtype)


def gated_deltanet(z, gates, conv_w, norm_w, y_mix, *, batch, seq, col0, width):
    d = GDN_HEAD_DIM
    c = GDN_BLOCK
    hps = GDN_HEADS_PER_STEP
    n_heads = width // d
    bw = hps * d
    nblk = seq // c
    cb0 = col0 // bw
    wb = width // bw
    ocb0 = (y_mix.shape[1] - width) // bw

    def zspec(part):
        return pl.BlockSpec((c, bw), lambda b, h, t: (b * nblk + t, cb0 + part * wb + h))

    def cspec(part):
        return pl.BlockSpec((CONV_WIDTH, bw), lambda b, h, t: (0, part * wb + h))

    return pl.pallas_call(
        functools.partial(_gdn_kernel, hps=hps, n_heads=n_heads),
        grid=(batch, n_heads // hps, nblk),
        in_specs=[zspec(0), zspec(1), zspec(2), zspec(3),
                  pl.BlockSpec((2 * n_heads, c), lambda b, h, t: (0, b * nblk + t)),
                  cspec(0), cspec(1), cspec(2),
                  pl.BlockSpec((1, d), lambda b, h, t: (0, 0)),
                  pl.BlockSpec(memory_space=pl.ANY)],
        out_specs=pl.BlockSpec((c, bw), lambda b, h, t: (b * nblk + t, ocb0 + h)),
        out_shape=jax.ShapeDtypeStruct(y_mix.shape, y_mix.dtype),
        input_output_aliases={9: 0},
        scratch_shapes=[pltpu.VMEM((3, SUBLANES, bw), F32), pltpu.VMEM((hps, d, d), F32)],
        compiler_params=_params("parallel", "parallel", "arbitrary"),
    )(z, z, z, z, gates, conv_w, conv_w, conv_w, norm_w.reshape(1, d), y_mix)


def _sgu_kernel(u_ref, v_ref, lng_ref, lnb_ref, ws_ref, bs_ref, o_ref, *, group):
    v = v_ref[...].astype(F32)
    mu = jnp.mean(v, axis=-1, keepdims=True)
    xc = v - mu
    vn = xc * lax.rsqrt(jnp.mean(xc * xc, axis=-1, keepdims=True) + EPS) * lng_ref[...] + lnb_ref[...]
    vb = vn.astype(BF16)
    ii = lax.broadcasted_iota(jnp.int32, (SGU_LEN, SGU_LEN), 0)
    jj = lax.broadcasted_iota(jnp.int32, (SGU_LEN, SGU_LEN), 1)
    chunk_bits = SGU_CHUNK.bit_length() - 1
    mask = (jj >> chunk_bits) <= (ii >> chunk_bits)
    for g in range(SGU_GROUPS):
        cols = slice(g * group, (g + 1) * group)
        ws = jnp.where(mask, ws_ref[g], 0.0).astype(BF16)
        for blk in range(SGU_BLOCKS_PER_STEP):
            rows = slice(blk * SGU_LEN, (blk + 1) * SGU_LEN)
            sv = jnp.dot(ws, vb[rows, cols], preferred_element_type=F32) + bs_ref[:, g:g + 1]
            o_ref[rows, cols] = (u_ref[rows, cols].astype(F32) * sv).astype(o_ref.dtype)


def spatial_gating(z, ln_g, ln_b, w_s, b_s):
    t, w2 = z.shape
    width = w2 // 2
    group = width // SGU_GROUPS
    tr = SGU_BLOCKS_PER_STEP * SGU_LEN
    assert t % tr == 0
    vec = pl.BlockSpec((1, width), lambda i: (0, 0))
    return pl.pallas_call(
        functools.partial(_sgu_kernel, group=group),
        grid=(t // tr,),
        in_specs=[pl.BlockSpec((tr, width), lambda i: (i, 0)),
                  pl.BlockSpec((tr, width), lambda i: (i, 1)),
                  vec, vec,
                  pl.BlockSpec((SGU_GROUPS, SGU_LEN, SGU_LEN), lambda i: (0, 0, 0)),
                  pl.BlockSpec((SGU_LEN, SGU_GROUPS), lambda i: (0, 0))],
        out_specs=pl.BlockSpec((tr, width), lambda i: (i, 0)),
        out_shape=jax.ShapeDtypeStruct((t, width), BF16),
        compiler_params=_params("parallel"),
    )(z, z, ln_g.reshape(1, width), ln_b.reshape(1, width), w_s, b_s.T)


def _mix_pool_gdn(h, i, w_in, pool_w, pool_scale, conv_w, a_log, dt_bias, norm_w, w_out, *, batch, seq):
    n_g, group, _ = pool_w.shape
    pool_width = n_g * group
    n_heads = a_log.shape[0]
    gdn_width = n_heads * GDN_HEAD_DIM
    main = pool_width + 4 * gdn_width
    w_in_t = jnp.swapaxes(w_in, 1, 2)
    z = matmul_wstream(h, w_in_t, i, n_cols=main, tm=1024, tn=1024, out_dtype=BF16,
                       w_transposed=True)
    gates = gdn_gates(h, w_in_t, i, main, a_log, dt_bias, tn=1024)
    y = pool_mixer(z, pool_w.astype(BF16), pool_scale, seq=seq, out_width=pool_width + gdn_width)
    y = gated_deltanet(z, gates, conv_w, norm_w, y, batch=batch, seq=seq, col0=pool_width,
                       width=gdn_width)
    return matmul_wstream(y, w_out, i, tm=1024, tn=1024, out_dtype=BF16)


def _mix_sgu(h, i, w_in, ln_g, ln_b, w_s, b_s, w_out):
    z = matmul_wstream(h, w_in, i, tm=1024, tn=1024, out_dtype=BF16, act="gelu")
    s = spatial_gating(z, ln_g, ln_b, w_s, b_s)
    return matmul_wstream(s, w_out, i, tm=1024, tn=1024, out_dtype=BF16)


def _swiglu(h, layer, w_gate, w_up, w_down):
    act, wd = ffn_up(h, w_gate, w_up, w_down, layer, tm=2048, tn=256, wd_slab=512)
    return matmul_wstream(act, wd, None, tm=512, tn=512, out_dtype=BF16)


def kernel(x, norm_mix_pre, norm_mix_post, norm_ffn_pre, norm_ffn_post, ab_w_in, pool_w, pool_scale, gdn_conv, gdn_a_log, gdn_dt_bias, gdn_norm, ab_w_out, sgu_w_in, sgu_ln_g, sgu_ln_b, sgu_w_s, sgu_b_s, sgu_w_out, ffn_w_gate, ffn_w_up, ffn_w_down):
    batch, seq, d_model = x.shape
    depth = norm_mix_pre.shape[0]
    xt = x.reshape(batch * seq, d_model)
    h = prenorm(xt, norm_mix_pre[0])
    for layer in range(depth):
        i = layer // 2
        if layer % 2 == 0:
            y = _mix_pool_gdn(h, i, ab_w_in, pool_w[i], pool_scale[i], gdn_conv[i], gdn_a_log[i],
                              gdn_dt_bias[i], gdn_norm[i], ab_w_out, batch=batch, seq=seq)
        else:
            y = _mix_sgu(h, i, sgu_w_in, sgu_ln_g[i], sgu_ln_b[i], sgu_w_s[i], sgu_b_s[i], sgu_w_out)
        xt, h = postnorm_residual(y, xt, norm_mix_post[layer], norm_ffn_pre[layer])
        y = _swiglu(h, layer, ffn_w_gate, ffn_w_up, ffn_w_down)
        nxt = norm_mix_pre[layer + 1] if layer + 1 < depth else None
        xt, h = postnorm_residual(y, xt, norm_ffn_post[layer], nxt)
    return xt.reshape(batch, seq, d_model)
```

```python
import functools

import jax
import jax.numpy as jnp
from jax import lax
from jax.experimental import pallas as pl
from jax.experimental.pallas import tpu as pltpu

F32 = jnp.float32
BF16 = jnp.bfloat16

EPS = 1e-6
POOL_WINDOWS = (2, 4, 8, 16)
POOL_HALO = 16
GDN_HEAD_DIM = 128
CONV_WIDTH = 4
GDN_BLOCK = 128
GDN_HEADS_PER_STEP = 16
SGU_GROUPS = 16
SGU_LEN = 128
SGU_CHUNK = 64
SGU_BLOCKS_PER_STEP = 4
LANES = 128
SUBLANES = 8
VMEM_LIMIT_BYTES = 62 * 1024 * 1024


def _params(*semantics):
    return pltpu.CompilerParams(dimension_semantics=semantics,
                                vmem_limit_bytes=VMEM_LIMIT_BYTES)


def _sigmoid(x):
    return 0.5 + 0.5 * jnp.tanh(0.5 * x)


def _silu(x):
    h = 0.5 * x
    return h + h * jnp.tanh(h)


def _gelu_tanh(x):
    c = 0.7978845608028654
    h = 0.5 * x
    return h + h * jnp.tanh(x * (c + (c * 0.044715) * (x * x)))


def _rms(x, g):
    return x * lax.rsqrt(jnp.mean(x * x, axis=-1, keepdims=True) + EPS) * g


def _prenorm_kernel(x_ref, g_ref, h_ref):
    h_ref[...] = _rms(x_ref[...], g_ref[...]).astype(h_ref.dtype)


def prenorm(x, g, *, tr=512):
    t, d = x.shape
    return pl.pallas_call(
        _prenorm_kernel,
        grid=(t // tr,),
        in_specs=[pl.BlockSpec((tr, d), lambda i: (i, 0)),
                  pl.BlockSpec((1, d), lambda i: (0, 0))],
        out_specs=pl.BlockSpec((tr, d), lambda i: (i, 0)),
        out_shape=jax.ShapeDtypeStruct((t, d), BF16),
        compiler_params=_params("parallel"),
    )(x, g.reshape(1, d))


def _postnorm_kernel(y_ref, x_ref, gpost_ref, *rest, with_next):
    xn = x_ref[...] + _rms(y_ref[...].astype(F32), gpost_ref[...])
    if with_next:
        gpre_ref, xo_ref, h_ref = rest
        h_ref[...] = _rms(xn, gpre_ref[...]).astype(h_ref.dtype)
    else:
        (xo_ref,) = rest
    xo_ref[...] = xn


def postnorm_residual(y, x, g_post, g_pre_next=None, *, tr=512):
    t, d = x.shape
    with_next = g_pre_next is not None
    row = pl.BlockSpec((tr, d), lambda i: (i, 0))
    vec = pl.BlockSpec((1, d), lambda i: (0, 0))
    in_specs = [row, row, vec]
    args = [y, x, g_post.reshape(1, d)]
    out_specs = [row]
    out_shape = [jax.ShapeDtypeStruct((t, d), F32)]
    if with_next:
        in_specs.append(vec)
        args.append(g_pre_next.reshape(1, d))
        out_specs.append(row)
        out_shape.append(jax.ShapeDtypeStruct((t, d), BF16))
    out = pl.pallas_call(
        functools.partial(_postnorm_kernel, with_next=with_next),
        grid=(t // tr,),
        in_specs=in_specs,
        out_specs=out_specs,
        out_shape=out_shape,
        compiler_params=_params("parallel"),
    )(*args)
    return (out[0], out[1]) if with_next else (out[0], None)


def _mm_stream_kernel(a_ref, w_ref, o_ref, *, act, w_transposed):
    w = w_ref[...].astype(BF16)
    contract_w = 1 if w_transposed else 0
    r = lax.dot_general(a_ref[...], w, (((1,), (contract_w,)), ((), ())),
                        preferred_element_type=F32)
    if act == "gelu":
        r = _gelu_tanh(r)
    o_ref[...] = r.astype(o_ref.dtype)


def matmul_wstream(a, w, layer, *, tm, tn, n_cols=None, out_dtype=F32, act=None,
                   w_transposed=False):
    m, kdim = a.shape
    if layer is None:
        n = w.shape[0] * w.shape[2]
        assert w.shape[2] == tn
        wspec = pl.BlockSpec((None, kdim, tn), lambda i, j: (j, 0, 0))
    elif w_transposed:
        n = w.shape[1] if n_cols is None else n_cols
        wspec = pl.BlockSpec((None, tn, kdim), lambda i, j: (layer, j, 0))
    else:
        n = w.shape[2] if n_cols is None else n_cols
        wspec = pl.BlockSpec((None, kdim, tn), lambda i, j: (layer, 0, j))
    assert m % tm == 0 and n % tn == 0
    return pl.pallas_call(
        functools.partial(_mm_stream_kernel, act=act, w_transposed=w_transposed),
        grid=(m // tm, n // tn),
        in_specs=[pl.BlockSpec((tm, kdim), lambda i, j: (i, 0)), wspec],
        out_specs=pl.BlockSpec((tm, tn), lambda i, j: (i, j)),
        out_shape=jax.ShapeDtypeStruct((m, n), out_dtype),
        compiler_params=_params("parallel", "parallel"),
    )(a, w)


def _gdn_gates_kernel(a_ref, w_ref, alog_ref, dtb_ref, o_ref, wb_ref, *, n_heads):
    @pl.when(pl.program_id(0) == 0)
    def _():
        wb_ref[...] = w_ref[...].astype(BF16)

    logits = lax.dot_general(wb_ref[...], a_ref[...], (((1,), (1,)), ((), ())),
                             preferred_element_type=F32)
    o_ref[:n_heads, :] = _sigmoid(logits[:n_heads, :])
    dl = logits[n_heads:2 * n_heads, :] + dtb_ref[...]
    softplus = jnp.maximum(dl, 0.0) + jnp.log1p(jnp.exp(-jnp.abs(dl)))
    s = -jnp.exp(alog_ref[...]) * softplus
    pos = lax.broadcasted_iota(jnp.int32, s.shape, 1) & (GDN_BLOCK - 1)
    shift = 1
    while shift < GDN_BLOCK:
        s = s + jnp.where(pos >= shift, pltpu.roll(s, shift, axis=1), 0.0)
        shift *= 2
    o_ref[n_heads:, :] = s


def gdn_gates(a, w_t, layer, row0, a_log, dt_bias, *, tn):
    n_heads = a_log.shape[0]
    m, kdim = a.shape
    assert row0 % LANES == 0 and 2 * n_heads <= LANES
    col = pl.BlockSpec((n_heads, 1), lambda i: (0, 0))
    return pl.pallas_call(
        functools.partial(_gdn_gates_kernel, n_heads=n_heads),
        grid=(m // tn,),
        in_specs=[pl.BlockSpec((tn, kdim), lambda i: (i, 0)),
                  pl.BlockSpec((None, LANES, kdim), lambda i: (layer, row0 // LANES, 0)),
                  col, col],
        out_specs=pl.BlockSpec((2 * n_heads, tn), lambda i: (0, i)),
        out_shape=jax.ShapeDtypeStruct((2 * n_heads, m), F32),
        scratch_shapes=[pltpu.VMEM((LANES, kdim), BF16)],
        compiler_params=_params("arbitrary"),
    )(a, w_t, a_log.astype(F32).reshape(n_heads, 1), dt_bias.astype(F32).reshape(n_heads, 1))


def _ffn_up_kernel(a_ref, wg_ref, wu_ref, wd_ref, o_ref, wd_bf_ref):
    @pl.when(pl.program_id(0) == 0)
    def _():
        slab = wd_bf_ref.shape[2]
        for p in range(wd_bf_ref.shape[0]):
            wd_bf_ref[p] = wd_ref[:, p * slab:(p + 1) * slab].astype(BF16)

    a = a_ref[...]
    g = jnp.dot(a, wg_ref[...].astype(BF16), preferred_element_type=F32)
    u = jnp.dot(a, wu_ref[...].astype(BF16), preferred_element_type=F32)
    o_ref[...] = (_silu(g) * u).astype(o_ref.dtype)


def ffn_up(a, wg, wu, wd, layer, *, tm, tn, wd_slab):
    m, kdim = a.shape
    n = wg.shape[2]
    d_out = wd.shape[2]
    assert m % tm == 0 and n % tn == 0 and d_out % wd_slab == 0
    nb = n // tn
    n_slabs = d_out // wd_slab
    wspec = pl.BlockSpec((None, kdim, tn), lambda i, j: (layer, 0, j))
    wd_rows = lambda i, j: jnp.where(i == 0, j, nb - 1)
    return pl.pallas_call(
        _ffn_up_kernel,
        grid=(m // tm, nb),
        in_specs=[pl.BlockSpec((tm, kdim), lambda i, j: (i, 0), pipeline_mode=pl.Buffered(1)),
                  wspec, wspec,
                  pl.BlockSpec((None, tn, d_out), lambda i, j: (layer, wd_rows(i, j), 0))],
        out_specs=[pl.BlockSpec((tm, tn), lambda i, j: (i, j)),
                   pl.BlockSpec((n_slabs, tn, wd_slab), lambda i, j: (0, wd_rows(i, j), 0))],
        out_shape=[jax.ShapeDtypeStruct((m, n), BF16),
                   jax.ShapeDtypeStruct((n_slabs, n, wd_slab), BF16)],
        compiler_params=_params("arbitrary", "arbitrary"),
    )(a, wg, wu, wd)


def _pool_kernel(halo_ref, x_ref, w_ref, scale_ref, o_ref, *, tb, seq, group):
    start = (pl.program_id(0) * tb) % seq
    keep_halo = (start > 0).astype(F32)
    pos = start + 1 + lax.broadcasted_iota(jnp.int32, (tb, 1), 0)
    for gi, win in enumerate(POOL_WINDOWS):
        cols = slice(gi * group, (gi + 1) * group)
        x = x_ref[:, cols].astype(F32)
        s = jnp.concatenate([halo_ref[:, cols].astype(F32) * keep_halo, x], axis=0)
        shift = 1
        while shift < win:
            s = s + pltpu.roll(s, shift, axis=0)
            shift *= 2
        cnt = jnp.minimum(pos, win).astype(F32)
        y = s[POOL_HALO:, :] / cnt - x
        r = jnp.dot(y.astype(BF16), w_ref[gi], preferred_element_type=F32)
        o_ref[:, cols] = (r * scale_ref[:, cols]).astype(o_ref.dtype)
    width = len(POOL_WINDOWS) * group
    o_ref[:, width:] = jnp.zeros((tb, o_ref.shape[1] - width), o_ref.dtype)


def pool_mixer(z, w_grp, scale, *, seq, out_width, tb=512):
    t = z.shape[0]
    n_g, group, _ = w_grp.shape
    width = n_g * group
    hb = tb // POOL_HALO
    return pl.pallas_call(
        functools.partial(_pool_kernel, tb=tb, seq=seq, group=group),
        grid=(t // tb,),
        in_specs=[pl.BlockSpec((POOL_HALO, width), lambda i: (jnp.maximum(i * hb - 1, 0), 0)),
                  pl.BlockSpec((tb, width), lambda i: (i, 0)),
                  pl.BlockSpec((n_g, group, group), lambda i: (0, 0, 0)),
                  pl.BlockSpec((1, width), lambda i: (0, 0))],
        out_specs=pl.BlockSpec((tb, out_width), lambda i: (i, 0)),
        out_shape=jax.ShapeDtypeStruct((t, out_width), BF16),
        compiler_params=_params("parallel"),
    )(z, z, w_grp, scale.reshape(1, width))


def _gdn_kernel(q_ref, k_ref, v_ref, gate_ref, gates_ref, cq_ref, ck_ref, cv_ref, nw_ref,
                y_hbm_ref, o_ref, halo_ref, state_ref, *, hps, n_heads):
    del y_hbm_ref
    c = GDN_BLOCK
    d = GDN_HEAD_DIM
    halo = SUBLANES
    heads = range(hps)
    t = pl.program_id(2)
    h0 = pl.program_id(1) * hps

    @pl.when(t == 0)
    def _():
        halo_ref[...] = jnp.zeros_like(halo_ref)
        state_ref[...] = jnp.zeros_like(state_ref)

    width = hps * d
    sub = lax.broadcasted_iota(jnp.int32, (1, halo, width), 1)

    def conv_silu(x_ref, idx, cw_ref):
        cur = x_ref[...].astype(F32)
        tiles = jnp.concatenate([halo_ref[idx], cur], axis=0).reshape(c // halo + 1, halo, width)
        w = cw_ref[...]
        acc = cur * w[CONV_WIDTH - 1:CONV_WIDTH, :]
        for back in range(1, CONV_WIDTH):
            rot = pltpu.roll(tiles, back, axis=1)
            tap = jnp.where(sub < back, rot[:-1], rot[1:]).reshape(c, width)
            acc = acc + tap * w[CONV_WIDTH - 1 - back:CONV_WIDTH - back, :]
        halo_ref[idx] = cur[c - halo:, :]
        return _silu(acc)

    qs = conv_silu(q_ref, 0, cq_ref)
    ks = conv_silu(k_ref, 1, ck_ref)
    vs = conv_silu(v_ref, 2, cv_ref)

    ii = lax.broadcasted_iota(jnp.int32, (c, c), 0)
    jj = lax.broadcasted_iota(jnp.int32, (c, c), 1)
    causal = ii >= jj
    strict = ii > jj

    def bf(a):
        return a.astype(BF16)

    def mm(a, b):
        return jnp.dot(a, b, preferred_element_type=F32)

    def mm_nt(a, b):
        return lax.dot_general(a, b, (((1,), (1,)), ((), ())), preferred_element_type=F32)

    def l2norm(x):
        return x * lax.rsqrt(jnp.sum(x * x, axis=-1, keepdims=True) + EPS)

    beta_rows = gates_ref[pl.ds(h0, hps), :]
    gc_rows = gates_ref[pl.ds(n_heads + h0, hps), :]

    decay, gc_last, kk, qk, rhs, qd_bf, kd_t_bf = [], [], [], [], [], [], []
    for j in heads:
        q = l2norm(qs[:, j * d:(j + 1) * d]) * (d ** -0.5)
        k = l2norm(ks[:, j * d:(j + 1) * d])
        v = vs[:, j * d:(j + 1) * d]
        beta = jnp.broadcast_to(beta_rows[j:j + 1, :], (c, c)).T
        gc_t = jnp.broadcast_to(gc_rows[j:j + 1, :], (c, c))
        gc = gc_t.T
        gamma = jnp.exp(gc)
        last = gc[c - 1:c, :]
        kb = k * beta
        k_bf = bf(k)
        kk.append(mm_nt(bf(kb), k_bf))
        qk.append(mm_nt(bf(q), k_bf))
        decay.append(jnp.exp(jnp.where(causal, gc - gc_t, -jnp.inf)))
        gc_last.append(last)
        rhs.append(jnp.concatenate([kb * gamma, v * beta], axis=1))
        qd_bf.append(bf(q * gamma))
        kd_t_bf.append(bf((k * jnp.exp(last - gc)).T))
    lmat = [jnp.where(strict, kk[j] * decay[j], 0.0) for j in heads]
    attn_bf = [bf(qk[j] * decay[j]) for j in heads]

    n = [-m for m in lmat]
    p_bf = [bf(m) for m in lmat]
    span = 2
    while span < c:
        p = [mm(m, m) for m in p_bf]
        p_bf = [bf(m) for m in p]
        n = [n[j] + p[j] + mm(bf(n[j]), p_bf[j]) for j in heads]
        span *= 2
    sol = [rhs[j] + mm(bf(n[j]), bf(rhs[j])) for j in heads]

    state = [state_ref[j] for j in heads]
    state_bf = [bf(m) for m in state]
    v_new = [sol[j][:, d:] - mm(bf(sol[j][:, :d]), state_bf[j]) for j in heads]
    v_new_bf = [bf(m) for m in v_new]
    o = [mm(qd_bf[j], state_bf[j]) + mm(attn_bf[j], v_new_bf[j]) for j in heads]
    for j in heads:
        state_ref[j] = jnp.exp(gc_last[j]) * state[j] + mm(kd_t_bf[j], v_new_bf[j])
    for j in heads:
        gate = gate_ref[:, j * d:(j + 1) * d].astype(F32)
        o_ref[:, j * d:(j + 1) * d] = (_rms(o[j], nw_ref[...]) * _silu(gate)).astype(o_ref.dtype)


def gated_deltanet(z, gates, conv_w, norm_w, y_mix, *, batch, seq, col0, width):
    d = GDN_HEAD_DIM
    c = GDN_BLOCK
    hps = GDN_HEADS_PER_STEP
    n_heads = width // d
    bw = hps * d
    nblk = seq // c
    cb0 = col0 // bw
    wb = width // bw
    ocb0 = (y_mix.shape[1] - width) // bw

    def zspec(part):
        return pl.BlockSpec((c, bw), lambda b, h, t: (b * nblk + t, cb0 + part * wb + h))

    def cspec(part):
        return pl.BlockSpec((CONV_WIDTH, bw), lambda b, h, t: (0, part * wb + h))

    return pl.pallas_call(
        functools.partial(_gdn_kernel, hps=hps, n_heads=n_heads),
        grid=(batch, n_heads // hps, nblk),
        in_specs=[zspec(0), zspec(1), zspec(2), zspec(3),
                  pl.BlockSpec((2 * n_heads, c), lambda b, h, t: (0, b * nblk + t)),
                  cspec(0), cspec(1), cspec(2),
                  pl.BlockSpec((1, d), lambda b, h, t: (0, 0)),
                  pl.BlockSpec(memory_space=pl.ANY)],
        out_specs=pl.BlockSpec((c, bw), lambda b, h, t: (b * nblk + t, ocb0 + h)),
        out_shape=jax.ShapeDtypeStruct(y_mix.shape, y_mix.dtype),
        input_output_aliases={9: 0},
        scratch_shapes=[pltpu.VMEM((3, SUBLANES, bw), F32), pltpu.VMEM((hps, d, d), F32)],
        compiler_params=_params("parallel", "parallel", "arbitrary"),
    )(z, z, z, z, gates, conv_w, conv_w, conv_w, norm_w.reshape(1, d), y_mix)


def _sgu_kernel(u_ref, v_ref, lng_ref, lnb_ref, ws_ref, bs_ref, o_ref, *, group):
    v = v_ref[...].astype(F32)
    mu = jnp.mean(v, axis=-1, keepdims=True)
    xc = v - mu
    vn = xc * lax.rsqrt(jnp.mean(xc * xc, axis=-1, keepdims=True) + EPS) * lng_ref[...] + lnb_ref[...]
    vb = vn.astype(BF16)
    ii = lax.broadcasted_iota(jnp.int32, (SGU_LEN, SGU_LEN), 0)
    jj = lax.broadcasted_iota(jnp.int32, (SGU_LEN, SGU_LEN), 1)
    chunk_bits = SGU_CHUNK.bit_length() - 1
    mask = (jj >> chunk_bits) <= (ii >> chunk_bits)
    for g in range(SGU_GROUPS):
        cols = slice(g * group, (g + 1) * group)
        ws = jnp.where(mask, ws_ref[g], 0.0).astype(BF16)
        for blk in range(SGU_BLOCKS_PER_STEP):
            rows = slice(blk * SGU_LEN, (blk + 1) * SGU_LEN)
            sv = jnp.dot(ws, vb[rows, cols], preferred_element_type=F32) + bs_ref[:, g:g + 1]
            o_ref[rows, cols] = (u_ref[rows, cols].astype(F32) * sv).astype(o_ref.dtype)


def spatial_gating(z, ln_g, ln_b, w_s, b_s):
    t, w2 = z.shape
    width = w2 // 2
    group = width // SGU_GROUPS
    tr = SGU_BLOCKS_PER_STEP * SGU_LEN
    assert t % tr == 0
    vec = pl.BlockSpec((1, width), lambda i: (0, 0))
    return pl.pallas_call(
        functools.partial(_sgu_kernel, group=group),
        grid=(t // tr,),
        in_specs=[pl.BlockSpec((tr, width), lambda i: (i, 0)),
                  pl.BlockSpec((tr, width), lambda i: (i, 1)),
                  vec, vec,
                  pl.BlockSpec((SGU_GROUPS, SGU_LEN, SGU_LEN), lambda i: (0, 0, 0)),
                  pl.BlockSpec((SGU_LEN, SGU_GROUPS), lambda i: (0, 0))],
        out_specs=pl.BlockSpec((tr, width), lambda i: (i, 0)),
        out_shape=jax.ShapeDtypeStruct((t, width), BF16),
        compiler_params=_params("parallel"),
    )(z, z, ln_g.reshape(1, width), ln_b.reshape(1, width), w_s, b_s.T)


def _mix_pool_gdn(h, i, w_in, pool_w, pool_scale, conv_w, a_log, dt_bias, norm_w, w_out, *, batch, seq):
    n_g, group, _ = pool_w.shape
    pool_width = n_g * group
    n_heads = a_log.shape[0]
    gdn_width = n_heads * GDN_HEAD_DIM
    main = pool_width + 4 * gdn_width
    w_in_t = jnp.swapaxes(w_in, 1, 2)
    z = matmul_wstream(h, w_in_t, i, n_cols=main, tm=1024, tn=1024, out_dtype=BF16,
                       w_transposed=True)
    gates = gdn_gates(h, w_in_t, i, main, a_log, dt_bias, tn=1024)
    y = pool_mixer(z, pool_w.astype(BF16), pool_scale, seq=seq, out_width=pool_width + gdn_width)
    y = gated_deltanet(z, gates, conv_w, norm_w, y, batch=batch, seq=seq, col0=pool_width,
                       width=gdn_width)
    return matmul_wstream(y, w_out, i, tm=1024, tn=1024, out_dtype=BF16)


def _mix_sgu(h, i, w_in, ln_g, ln_b, w_s, b_s, w_out):
    z = matmul_wstream(h, w_in, i, tm=1024, tn=1024, out_dtype=BF16, act="gelu")
    s = spatial_gating(z, ln_g, ln_b, w_s, b_s)
    return matmul_wstream(s, w_out, i, tm=1024, tn=1024, out_dtype=BF16)


def _swiglu(h, layer, w_gate, w_up, w_down):
    act, wd = ffn_up(h, w_gate, w_up, w_down, layer, tm=2048, tn=256, wd_slab=512)
    return matmul_wstream(act, wd, None, tm=512, tn=512, out_dtype=BF16)


def kernel(x, norm_mix_pre, norm_mix_post, norm_ffn_pre, norm_ffn_post, ab_w_in, pool_w, pool_scale, gdn_conv, gdn_a_log, gdn_dt_bias, gdn_norm, ab_w_out, sgu_w_in, sgu_ln_g, sgu_ln_b, sgu_w_s, sgu_b_s, sgu_w_out, ffn_w_gate, ffn_w_up, ffn_w_down):
    batch, seq, d_model = x.shape
    depth = norm_mix_pre.shape[0]
    xt = x.reshape(batch * seq, d_model)
    h = prenorm(xt, norm_mix_pre[0])
    for layer in range(depth):
        i = layer // 2
        if layer % 2 == 0:
            y = _mix_pool_gdn(h, i, ab_w_in, pool_w[i], pool_scale[i], gdn_conv[i], gdn_a_log[i],
                              gdn_dt_bias[i], gdn_norm[i], ab_w_out, batch=batch, seq=seq)
        else:
            y = _mix_sgu(h, i, sgu_w_in, sgu_ln_g[i], sgu_ln_b[i], sgu_w_s[i], sgu_b_s[i], sgu_w_out)
        xt, h = postnorm_residual(y, xt, norm_mix_post[layer], norm_ffn_pre[layer])
        y = _swiglu(h, layer, ffn_w_gate, ffn_w_up, ffn_w_down)
        nxt = norm_mix_pre[layer + 1] if layer + 1 < depth else None
        xt, h = postnorm_residual(y, xt, norm_ffn_post[layer], nxt)
    return xt.reshape(batch, seq, d_model)
```

```python
import functools

import jax
import jax.numpy as jnp
from jax import lax
from jax.experimental import pallas as pl
from jax.experimental.pallas import tpu as pltpu

F32 = jnp.float32
BF16 = jnp.bfloat16

EPS = 1e-6
POOL_WINDOWS = (2, 4, 8, 16)
POOL_HALO = 16
GDN_HEAD_DIM = 128
CONV_WIDTH = 4
GDN_BLOCK = 128
GDN_HEADS_PER_STEP = 16
SGU_GROUPS = 16
SGU_LEN = 128
SGU_CHUNK = 64
SGU_BLOCKS_PER_STEP = 4
LANES = 128
SUBLANES = 8
VMEM_LIMIT_BYTES = 62 * 1024 * 1024


def _params(*semantics):
    return pltpu.CompilerParams(dimension_semantics=semantics,
                                vmem_limit_bytes=VMEM_LIMIT_BYTES)


def _sigmoid(x):
    return 0.5 + 0.5 * jnp.tanh(0.5 * x)


def _silu(x):
    h = 0.5 * x
    return h + h * jnp.tanh(h)


def _gelu_tanh(x):
    c = 0.7978845608028654
    h = 0.5 * x
    return h + h * jnp.tanh(x * (c + (c * 0.044715) * (x * x)))


def _rms(x, g):
    return x * lax.rsqrt(jnp.mean(x * x, axis=-1, keepdims=True) + EPS) * g


def _postnorm_kernel(y_ref, x_ref, gpost_ref, *rest, with_next):
    xn = x_ref[...] + _rms(y_ref[...].astype(F32), gpost_ref[...])
    if with_next:
        gpre_ref, xo_ref, h_ref = rest
        h_ref[...] = _rms(xn, gpre_ref[...]).astype(h_ref.dtype)
    else:
        (xo_ref,) = rest
    xo_ref[...] = xn


def postnorm_residual(y, x, g_post, g_pre_next=None, *, tr=512):
    t, d = x.shape
    with_next = g_pre_next is not None
    row = pl.BlockSpec((tr, d), lambda i: (i, 0))
    vec = pl.BlockSpec((1, d), lambda i: (0, 0))
    in_specs = [row, row, vec]
    args = [y, x, g_post.reshape(1, d)]
    out_specs = [row]
    out_shape = [jax.ShapeDtypeStruct((t, d), F32)]
    if with_next:
        in_specs.append(vec)
        args.append(g_pre_next.reshape(1, d))
        out_specs.append(row)
        out_shape.append(jax.ShapeDtypeStruct((t, d), BF16))
    out = pl.pallas_call(
        functools.partial(_postnorm_kernel, with_next=with_next),
        grid=(t // tr,),
        in_specs=in_specs,
        out_specs=out_specs,
        out_shape=out_shape,
        compiler_params=_params("parallel"),
    )(*args)
    return (out[0], out[1]) if with_next else (out[0], None)


def _mm_stream_kernel(a_ref, w_ref, o_ref, *, act, w_transposed):
    w = w_ref[...].astype(BF16)
    contract_w = 1 if w_transposed else 0
    r = lax.dot_general(a_ref[...], w, (((1,), (contract_w,)), ((), ())),
                        preferred_element_type=F32)
    if act == "gelu":
        r = _gelu_tanh(r)
    o_ref[...] = r.astype(o_ref.dtype)


def matmul_wstream(a, w, layer, *, tm, tn, n_cols=None, out_dtype=F32, act=None,
                   w_transposed=False):
    m, kdim = a.shape
    if layer is None:
        n = w.shape[0] * w.shape[2]
        assert w.shape[2] == tn
        wspec = pl.BlockSpec((None, kdim, tn), lambda i, j: (j, 0, 0))
    elif w_transposed:
        n = w.shape[1] if n_cols is None else n_cols
        wspec = pl.BlockSpec((None, tn, kdim), lambda i, j: (layer, j, 0))
    else:
        n = w.shape[2] if n_cols is None else n_cols
        wspec = pl.BlockSpec((None, kdim, tn), lambda i, j: (layer, 0, j))
    assert m % tm == 0 and n % tn == 0
    return pl.pallas_call(
        functools.partial(_mm_stream_kernel, act=act, w_transposed=w_transposed),
        grid=(m // tm, n // tn),
        in_specs=[pl.BlockSpec((tm, kdim), lambda i, j: (i, 0)), wspec],
        out_specs=pl.BlockSpec((tm, tn), lambda i, j: (i, j)),
        out_shape=jax.ShapeDtypeStruct((m, n), out_dtype),
        compiler_params=_params("parallel", "parallel"),
    )(a, w)


def _write_gdn_gates(w_ref, wb_ref, a, alog_ref, dtb_ref, o_ref, n_heads):
    @pl.when(pl.program_id(0) == 0)
    def _():
        wb_ref[...] = w_ref[...].astype(BF16)

    logits = lax.dot_general(wb_ref[...], a, (((1,), (1,)), ((), ())),
                             preferred_element_type=F32)
    o_ref[:n_heads, :] = _sigmoid(logits[:n_heads, :])
    dl = logits[n_heads:2 * n_heads, :] + dtb_ref[...]
    softplus = jnp.maximum(dl, 0.0) + jnp.log1p(jnp.exp(-jnp.abs(dl)))
    s = -jnp.exp(alog_ref[...]) * softplus
    pos = lax.broadcasted_iota(jnp.int32, s.shape, 1) & (GDN_BLOCK - 1)
    shift = 1
    while shift < GDN_BLOCK:
        s = s + jnp.where(pos >= shift, pltpu.roll(s, shift, axis=1), 0.0)
        shift *= 2
    o_ref[n_heads:, :] = s


def _gdn_gates_kernel(a_ref, w_ref, alog_ref, dtb_ref, o_ref, wb_ref, *, n_heads):
    _write_gdn_gates(w_ref, wb_ref, a_ref[...], alog_ref, dtb_ref, o_ref, n_heads)


def _prenorm_gates_kernel(x_ref, g_ref, w_ref, alog_ref, dtb_ref, h_ref, o_ref, wb_ref, *, n_heads):
    h = _rms(x_ref[...], g_ref[...]).astype(h_ref.dtype)
    h_ref[...] = h
    _write_gdn_gates(w_ref, wb_ref, h, alog_ref, dtb_ref, o_ref, n_heads)


def prenorm_with_gates(x, g, w_t, layer, row0, a_log, dt_bias, *, tr=512):
    t, d = x.shape
    n_heads = a_log.shape[0]
    assert row0 % LANES == 0 and 2 * n_heads <= LANES
    col = pl.BlockSpec((n_heads, 1), lambda i: (0, 0))
    return pl.pallas_call(
        functools.partial(_prenorm_gates_kernel, n_heads=n_heads),
        grid=(t // tr,),
        in_specs=[pl.BlockSpec((tr, d), lambda i: (i, 0)),
                  pl.BlockSpec((1, d), lambda i: (0, 0)),
                  pl.BlockSpec((None, LANES, d), lambda i: (layer, row0 // LANES, 0)),
                  col, col],
        out_specs=[pl.BlockSpec((tr, d), lambda i: (i, 0)),
                   pl.BlockSpec((2 * n_heads, tr), lambda i: (0, i))],
        out_shape=[jax.ShapeDtypeStruct((t, d), BF16),
                   jax.ShapeDtypeStruct((2 * n_heads, t), F32)],
        scratch_shapes=[pltpu.VMEM((LANES, d), BF16)],
        compiler_params=_params("arbitrary"),
    )(x, g.reshape(1, d), w_t, a_log.astype(F32).reshape(n_heads, 1),
      dt_bias.astype(F32).reshape(n_heads, 1))


def gdn_gates(a, w_t, layer, row0, a_log, dt_bias, *, tn):
    n_heads = a_log.shape[0]
    m, kdim = a.shape
    assert row0 % LANES == 0 and 2 * n_heads <= LANES
    col = pl.BlockSpec((n_heads, 1), lambda i: (0, 0))
    return pl.pallas_call(
        functools.partial(_gdn_gates_kernel, n_heads=n_heads),
        grid=(m // tn,),
        in_specs=[pl.BlockSpec((tn, kdim), lambda i: (i, 0)),
                  pl.BlockSpec((None, LANES, kdim), lambda i: (layer, row0 // LANES, 0)),
                  col, col],
        out_specs=pl.BlockSpec((2 * n_heads, tn), lambda i: (0, i)),
        out_shape=jax.ShapeDtypeStruct((2 * n_heads, m), F32),
        scratch_shapes=[pltpu.VMEM((LANES, kdim), BF16)],
        compiler_params=_params("arbitrary"),
    )(a, w_t, a_log.astype(F32).reshape(n_heads, 1), dt_bias.astype(F32).reshape(n_heads, 1))


def _ffn_up_kernel(a_ref, wg_ref, wu_ref, wd_ref, o_ref, wd_bf_ref):
    @pl.when(pl.program_id(0) == 0)
    def _():
        slab = wd_bf_ref.shape[2]
        for p in range(wd_bf_ref.shape[0]):
            wd_bf_ref[p] = wd_ref[:, p * slab:(p + 1) * slab].astype(BF16)

    a = a_ref[...]
    g = jnp.dot(a, wg_ref[...].astype(BF16), preferred_element_type=F32)
    u = jnp.dot(a, wu_ref[...].astype(BF16), preferred_element_type=F32)
    o_ref[...] = (_silu(g) * u).astype(o_ref.dtype)


def ffn_up(a, wg, wu, wd, layer, *, tm, tn, wd_slab):
    m, kdim = a.shape
    n = wg.shape[2]
    d_out = wd.shape[2]
    assert m % tm == 0 and n % tn == 0 and d_out % wd_slab == 0
    nb = n // tn
    n_slabs = d_out // wd_slab
    wspec = pl.BlockSpec((None, kdim, tn), lambda i, j: (layer, 0, j))
    wd_rows = lambda i, j: jnp.where(i == 0, j, nb - 1)
    return pl.pallas_call(
        _ffn_up_kernel,
        grid=(m // tm, nb),
        in_specs=[pl.BlockSpec((tm, kdim), lambda i, j: (i, 0), pipeline_mode=pl.Buffered(1)),
                  wspec, wspec,
                  pl.BlockSpec((None, tn, d_out), lambda i, j: (layer, wd_rows(i, j), 0))],
        out_specs=[pl.BlockSpec((tm, tn), lambda i, j: (i, j)),
                   pl.BlockSpec((n_slabs, tn, wd_slab), lambda i, j: (0, wd_rows(i, j), 0))],
        out_shape=[jax.ShapeDtypeStruct((m, n), BF16),
                   jax.ShapeDtypeStruct((n_slabs, n, wd_slab), BF16)],
        compiler_params=_params("arbitrary", "arbitrary"),
    )(a, wg, wu, wd)


def _pool_kernel(halo_ref, x_ref, w_ref, scale_ref, o_ref, *, tb, seq, group):
    start = (pl.program_id(0) * tb) % seq
    keep_halo = (start > 0).astype(F32)
    pos = start + 1 + lax.broadcasted_iota(jnp.int32, (tb, 1), 0)
    for gi, win in enumerate(POOL_WINDOWS):
        cols = slice(gi * group, (gi + 1) * group)
        x = x_ref[:, cols].astype(F32)
        s = jnp.concatenate([halo_ref[:, cols].astype(F32) * keep_halo, x], axis=0)
        shift = 1
        while shift < win:
            s = s + pltpu.roll(s, shift, axis=0)
            shift *= 2
        cnt = jnp.minimum(pos, win).astype(F32)
        y = s[POOL_HALO:, :] / cnt - x
        r = jnp.dot(y.astype(BF16), w_ref[gi], preferred_element_type=F32)
        o_ref[:, cols] = (r * scale_ref[:, cols]).astype(o_ref.dtype)
    width = len(POOL_WINDOWS) * group
    o_ref[:, width:] = jnp.zeros((tb, o_ref.shape[1] - width), o_ref.dtype)


def pool_mixer(z, w_grp, scale, *, seq, out_width, tb=512):
    t = z.shape[0]
    n_g, group, _ = w_grp.shape
    width = n_g * group
    hb = tb // POOL_HALO
    return pl.pallas_call(
        functools.partial(_pool_kernel, tb=tb, seq=seq, group=group),
        grid=(t // tb,),
        in_specs=[pl.BlockSpec((POOL_HALO, width), lambda i: (jnp.maximum(i * hb - 1, 0), 0)),
                  pl.BlockSpec((tb, width), lambda i: (i, 0)),
                  pl.BlockSpec((n_g, group, group), lambda i: (0, 0, 0)),
                  pl.BlockSpec((1, width), lambda i: (0, 0))],
        out_specs=pl.BlockSpec((tb, out_width), lambda i: (i, 0)),
        out_shape=jax.ShapeDtypeStruct((t, out_width), BF16),
        compiler_params=_params("parallel"),
    )(z, z, w_grp, scale.reshape(1, width))


def _gdn_kernel(q_ref, k_ref, v_ref, gate_ref, gates_ref, cq_ref, ck_ref, cv_ref, nw_ref,
                y_hbm_ref, o_ref, halo_ref, state_ref, *, hps, n_heads):
    del y_hbm_ref
    c = GDN_BLOCK
    d = GDN_HEAD_DIM
    halo = SUBLANES
    heads = range(hps)
    t = pl.program_id(2)
    h0 = pl.program_id(1) * hps

    @pl.when(t == 0)
    def _():
        halo_ref[...] = jnp.zeros_like(halo_ref)
        state_ref[...] = jnp.zeros_like(state_ref)

    width = hps * d
    sub = lax.broadcasted_iota(jnp.int32, (1, halo, width), 1)

    def conv_silu(x_ref, idx, cw_ref):
        cur = x_ref[...].astype(F32)
        tiles = jnp.concatenate([halo_ref[idx], cur], axis=0).reshape(c // halo + 1, halo, width)
        w = cw_ref[...]
        acc = cur * w[CONV_WIDTH - 1:CONV_WIDTH, :]
        for back in range(1, CONV_WIDTH):
            rot = pltpu.roll(tiles, back, axis=1)
            tap = jnp.where(sub < back, rot[:-1], rot[1:]).reshape(c, width)
            acc = acc + tap * w[CONV_WIDTH - 1 - back:CONV_WIDTH - back, :]
        halo_ref[idx] = cur[c - halo:, :]
        return _silu(acc)

    qs = conv_silu(q_ref, 0, cq_ref)
    ks = conv_silu(k_ref, 1, ck_ref)
    vs = conv_silu(v_ref, 2, cv_ref)

    ii = lax.broadcasted_iota(jnp.int32, (c, c), 0)
    jj = lax.broadcasted_iota(jnp.int32, (c, c), 1)
    causal = ii >= jj
    strict = ii > jj

    def bf(a):
        return a.astype(BF16)

    def mm(a, b):
        return jnp.dot(a, b, preferred_element_type=F32)

    def mm_nt(a, b):
        return lax.dot_general(a, b, (((1,), (1,)), ((), ())), preferred_element_type=F32)

    def l2norm(x):
        return x * lax.rsqrt(jnp.sum(x * x, axis=-1, keepdims=True) + EPS)

    beta_rows = gates_ref[pl.ds(h0, hps), :]
    gc_rows = gates_ref[pl.ds(n_heads + h0, hps), :]

    decay, gc_last, kk, qk, rhs, qd_bf, kd_t_bf = [], [], [], [], [], [], []
    for j in heads:
        q = l2norm(qs[:, j * d:(j + 1) * d]) * (d ** -0.5)
        k = l2norm(ks[:, j * d:(j + 1) * d])
        v = vs[:, j * d:(j + 1) * d]
        beta = jnp.broadcast_to(beta_rows[j:j + 1, :], (c, c)).T
        gc_t = jnp.broadcast_to(gc_rows[j:j + 1, :], (c, c))
        gc = gc_t.T
        gamma = jnp.exp(gc)
        last = gc[c - 1:c, :]
        kb = k * beta
        k_bf = bf(k)
        kk.append(mm_nt(bf(kb), k_bf))
        qk.append(mm_nt(bf(q), k_bf))
        decay.append(jnp.exp(jnp.where(causal, gc - gc_t, -jnp.inf)))
        gc_last.append(last)
        rhs.append(jnp.concatenate([kb * gamma, v * beta], axis=1))
        qd_bf.append(bf(q * gamma))
        kd_t_bf.append(bf((k * jnp.exp(last - gc)).T))
    lmat = [jnp.where(strict, kk[j] * decay[j], 0.0) for j in heads]
    attn_bf = [bf(qk[j] * decay[j]) for j in heads]

    n = [-m for m in lmat]
    p_bf = [bf(m) for m in lmat]
    span = 2
    while span < c:
        p = [mm(m, m) for m in p_bf]
        p_bf = [bf(m) for m in p]
        n = [n[j] + p[j] + mm(bf(n[j]), p_bf[j]) for j in heads]
        span *= 2
    sol = [rhs[j] + mm(bf(n[j]), bf(rhs[j])) for j in heads]

    state = [state_ref[j] for j in heads]
    state_bf = [bf(m) for m in state]
    v_new = [sol[j][:, d:] - mm(bf(sol[j][:, :d]), state_bf[j]) for j in heads]
    v_new_bf = [bf(m) for m in v_new]
    o = [mm(qd_bf[j], state_bf[j]) + mm(attn_bf[j], v_new_bf[j]) for j in heads]
    for j in heads:
        state_ref[j] = jnp.exp(gc_last[j]) * state[j] + mm(kd_t_bf[j], v_new_bf[j])
    for j in heads:
        gate = gate_ref[:, j * d:(j + 1) * d].astype(F32)
        o_ref[:, j * d:(j + 1) * d] = (_rms(o[j], nw_ref[...]) * _silu(gate)).astype(o_ref.dtype)


def gated_deltanet(z, gates, conv_w, norm_w, y_mix, *, batch, seq, col0, width):
    d = GDN_HEAD_DIM
    c = GDN_BLOCK
    hps = GDN_HEADS_PER_STEP
    n_heads = width // d
    bw = hps * d
    nblk = seq // c
    cb0 = col0 // bw
    wb = width // bw
    ocb0 = (y_mix.shape[1] - width) // bw

    def zspec(part):
        return pl.BlockSpec((c, bw), lambda b, h, t: (b * nblk + t, cb0 + part * wb + h))

    def cspec(part):
        return pl.BlockSpec((CONV_WIDTH, bw), lambda b, h, t: (0, part * wb + h))

    return pl.pallas_call(
        functools.partial(_gdn_kernel, hps=hps, n_heads=n_heads),
        grid=(batch, n_heads // hps, nblk),
        in_specs=[zspec(0), zspec(1), zspec(2), zspec(3),
                  pl.BlockSpec((2 * n_heads, c), lambda b, h, t: (0, b * nblk + t)),
                  cspec(0), cspec(1), cspec(2),
                  pl.BlockSpec((1, d), lambda b, h, t: (0, 0)),
                  pl.BlockSpec(memory_space=pl.ANY)],
        out_specs=pl.BlockSpec((c, bw), lambda b, h, t: (b * nblk + t, ocb0 + h)),
        out_shape=jax.ShapeDtypeStruct(y_mix.shape, y_mix.dtype),
        input_output_aliases={9: 0},
        scratch_shapes=[pltpu.VMEM((3, SUBLANES, bw), F32), pltpu.VMEM((hps, d, d), F32)],
        compiler_params=_params("parallel", "parallel", "arbitrary"),
    )(z, z, z, z, gates, conv_w, conv_w, conv_w, norm_w.reshape(1, d), y_mix)


def _sgu_kernel(u_ref, v_ref, lng_ref, lnb_ref, ws_ref, bs_ref, o_ref, *, group):
    v = v_ref[...].astype(F32)
    mu = jnp.mean(v, axis=-1, keepdims=True)
    xc = v - mu
    vn = xc * lax.rsqrt(jnp.mean(xc * xc, axis=-1, keepdims=True) + EPS) * lng_ref[...] + lnb_ref[...]
    vb = vn.astype(BF16)
    ii = lax.broadcasted_iota(jnp.int32, (SGU_LEN, SGU_LEN), 0)
    jj = lax.broadcasted_iota(jnp.int32, (SGU_LEN, SGU_LEN), 1)
    chunk_bits = SGU_CHUNK.bit_length() - 1
    mask = (jj >> chunk_bits) <= (ii >> chunk_bits)
    for g in range(SGU_GROUPS):
        cols = slice(g * group, (g + 1) * group)
        ws = jnp.where(mask, ws_ref[g], 0.0).astype(BF16)
        for blk in range(SGU_BLOCKS_PER_STEP):
            rows = slice(blk * SGU_LEN, (blk + 1) * SGU_LEN)
            sv = jnp.dot(ws, vb[rows, cols], preferred_element_type=F32) + bs_ref[:, g:g + 1]
            o_ref[rows, cols] = (u_ref[rows, cols].astype(F32) * sv).astype(o_ref.dtype)


def spatial_gating(z, ln_g, ln_b, w_s, b_s):
    t, w2 = z.shape
    width = w2 // 2
    group = width // SGU_GROUPS
    tr = SGU_BLOCKS_PER_STEP * SGU_LEN
    assert t % tr == 0
    vec = pl.BlockSpec((1, width), lambda i: (0, 0))
    return pl.pallas_call(
        functools.partial(_sgu_kernel, group=group),
        grid=(t // tr,),
        in_specs=[pl.BlockSpec((tr, width), lambda i: (i, 0)),
                  pl.BlockSpec((tr, width), lambda i: (i, 1)),
                  vec, vec,
                  pl.BlockSpec((SGU_GROUPS, SGU_LEN, SGU_LEN), lambda i: (0, 0, 0)),
                  pl.BlockSpec((SGU_LEN, SGU_GROUPS), lambda i: (0, 0))],
        out_specs=pl.BlockSpec((tr, width), lambda i: (i, 0)),
        out_shape=jax.ShapeDtypeStruct((t, width), BF16),
        compiler_params=_params("parallel"),
    )(z, z, ln_g.reshape(1, width), ln_b.reshape(1, width), w_s, b_s.T)


def _gate_row0(pool_w, a_log):
    n_g, group, _ = pool_w.shape
    return n_g * group + 4 * a_log.shape[0] * GDN_HEAD_DIM


def _mix_pool_gdn(h, gates, i, w_in_t, pool_w, pool_scale, conv_w, a_log, dt_bias, norm_w, w_out, *,
                  batch, seq):
    n_g, group, _ = pool_w.shape
    pool_width = n_g * group
    gdn_width = a_log.shape[0] * GDN_HEAD_DIM
    main = _gate_row0(pool_w, a_log)
    z = matmul_wstream(h, w_in_t, i, n_cols=main, tm=1024, tn=1024, out_dtype=BF16,
                       w_transposed=True)
    if gates is None:
        gates = gdn_gates(h, w_in_t, i, main, a_log, dt_bias, tn=1024)
    y = pool_mixer(z, pool_w.astype(BF16), pool_scale, seq=seq, out_width=pool_width + gdn_width)
    y = gated_deltanet(z, gates, conv_w, norm_w, y, batch=batch, seq=seq, col0=pool_width,
                       width=gdn_width)
    return matmul_wstream(y, w_out, i, tm=1024, tn=1024, out_dtype=BF16)


def _mix_sgu(h, i, w_in, ln_g, ln_b, w_s, b_s, w_out):
    z = matmul_wstream(h, w_in, i, tm=1024, tn=1024, out_dtype=BF16, act="gelu")
    s = spatial_gating(z, ln_g, ln_b, w_s, b_s)
    return matmul_wstream(s, w_out, i, tm=1024, tn=1024, out_dtype=BF16)


def _swiglu(h, layer, w_gate, w_up, w_down):
    act, wd = ffn_up(h, w_gate, w_up, w_down, layer, tm=2048, tn=256, wd_slab=512)
    return matmul_wstream(act, wd, None, tm=512, tn=512, out_dtype=BF16)


def kernel(x, norm_mix_pre, norm_mix_post, norm_ffn_pre, norm_ffn_post, ab_w_in, pool_w, pool_scale, gdn_conv, gdn_a_log, gdn_dt_bias, gdn_norm, ab_w_out, sgu_w_in, sgu_ln_g, sgu_ln_b, sgu_w_s, sgu_b_s, sgu_w_out, ffn_w_gate, ffn_w_up, ffn_w_down):
    batch, seq, d_model = x.shape
    depth = norm_mix_pre.shape[0]
    xt = x.reshape(batch * seq, d_model)
    ab_w_in_t = jnp.swapaxes(ab_w_in, 1, 2)
    h, gates = prenorm_with_gates(xt, norm_mix_pre[0], ab_w_in_t, 0,
                                  _gate_row0(pool_w[0], gdn_a_log[0]), gdn_a_log[0], gdn_dt_bias[0])
    for layer in range(depth):
        i = layer // 2
        if layer % 2 == 0:
            y = _mix_pool_gdn(h, gates if layer == 0 else None, i, ab_w_in_t, pool_w[i],
                              pool_scale[i], gdn_conv[i], gdn_a_log[i], gdn_dt_bias[i],
                              gdn_norm[i], ab_w_out, batch=batch, seq=seq)
        else:
            y = _mix_sgu(h, i, sgu_w_in, sgu_ln_g[i], sgu_ln_b[i], sgu_w_s[i], sgu_b_s[i], sgu_w_out)
        xt, h = postnorm_residual(y, xt, norm_mix_post[layer], norm_ffn_pre[layer])
        y = _swiglu(h, layer, ffn_w_gate, ffn_w_up, ffn_w_down)
        nxt = norm_mix_pre[layer + 1] if layer + 1 < depth else None
        xt, h = postnorm_residual(y, xt, norm_ffn_post[layer], nxt)
    return xt.reshape(batch, seq, d_model)
```

```python
import functools

import jax
import jax.numpy as jnp
from jax import lax
from jax.experimental import pallas as pl
from jax.experimental.pallas import tpu as pltpu

F32 = jnp.float32
BF16 = jnp.bfloat16

EPS = 1e-6
POOL_WINDOWS = (2, 4, 8, 16)
POOL_HALO = 16
GDN_HEAD_DIM = 128
CONV_WIDTH = 4
GDN_BLOCK = 128
GDN_HEADS_PER_STEP = 16
SGU_GROUPS = 16
SGU_LEN = 128
SGU_CHUNK = 64
SGU_BLOCKS_PER_STEP = 4
LANES = 128
SUBLANES = 8
VMEM_LIMIT_BYTES = 62 * 1024 * 1024


def _params(*semantics):
    return pltpu.CompilerParams(dimension_semantics=semantics,
                                vmem_limit_bytes=VMEM_LIMIT_BYTES)


def _sigmoid(x):
    return 0.5 + 0.5 * jnp.tanh(0.5 * x)


def _silu(x):
    h = 0.5 * x
    return h + h * jnp.tanh(h)


def _gelu_tanh(x):
    c = 0.7978845608028654
    h = 0.5 * x
    return h + h * jnp.tanh(x * (c + (c * 0.044715) * (x * x)))


def _rms(x, g):
    return x * lax.rsqrt(jnp.mean(x * x, axis=-1, keepdims=True) + EPS) * g


def _postnorm_kernel(y_ref, x_ref, gpost_ref, *rest, with_next):
    xn = x_ref[...] + _rms(y_ref[...].astype(F32), gpost_ref[...])
    if with_next:
        gpre_ref, xo_ref, h_ref = rest
        h_ref[...] = _rms(xn, gpre_ref[...]).astype(h_ref.dtype)
    else:
        (xo_ref,) = rest
    xo_ref[...] = xn


def postnorm_residual(y, x, g_post, g_pre_next=None, *, tr=512):
    t, d = x.shape
    with_next = g_pre_next is not None
    row = pl.BlockSpec((tr, d), lambda i: (i, 0))
    vec = pl.BlockSpec((1, d), lambda i: (0, 0))
    in_specs = [row, row, vec]
    args = [y, x, g_post.reshape(1, d)]
    out_specs = [row]
    out_shape = [jax.ShapeDtypeStruct((t, d), F32)]
    if with_next:
        in_specs.append(vec)
        args.append(g_pre_next.reshape(1, d))
        out_specs.append(row)
        out_shape.append(jax.ShapeDtypeStruct((t, d), BF16))
    out = pl.pallas_call(
        functools.partial(_postnorm_kernel, with_next=with_next),
        grid=(t // tr,),
        in_specs=in_specs,
        out_specs=out_specs,
        out_shape=out_shape,
        compiler_params=_params("parallel"),
    )(*args)
    return (out[0], out[1]) if with_next else (out[0], None)


def _mm_stream_kernel(a_ref, w_ref, o_ref, *, act, w_transposed):
    w = w_ref[...].astype(BF16)
    contract_w = 1 if w_transposed else 0
    r = lax.dot_general(a_ref[...], w, (((1,), (contract_w,)), ((), ())),
                        preferred_element_type=F32)
    if act == "gelu":
        r = _gelu_tanh(r)
    o_ref[...] = r.astype(o_ref.dtype)


def matmul_wstream(a, w, layer, *, tm, tn, n_cols=None, out_dtype=F32, act=None,
                   w_transposed=False, w_resident=False):
    m, kdim = a.shape
    if layer is None:
        n = w.shape[0] * w.shape[2]
        assert w.shape[2] == tn
        wspec = pl.BlockSpec((None, kdim, tn), lambda i, j: (j, 0, 0))
    elif w_transposed:
        n = w.shape[1] if n_cols is None else n_cols
        wspec = pl.BlockSpec((None, tn, kdim), lambda i, j: (layer, j, 0))
    else:
        n = w.shape[2] if n_cols is None else n_cols
        wspec = pl.BlockSpec((None, kdim, tn), lambda i, j: (layer, 0, j))
    assert m % tm == 0 and n % tn == 0
    aspec = pl.BlockSpec((tm, kdim), lambda i, j: (i, 0))
    ospec = pl.BlockSpec((tm, tn), lambda i, j: (i, j))
    grid = (m // tm, n // tn)
    if w_resident:
        grid = grid[::-1]
        aspec, wspec, ospec = [pl.BlockSpec(s.block_shape, lambda j, i, f=s.index_map: f(i, j))
                               for s in (aspec, wspec, ospec)]
    return pl.pallas_call(
        functools.partial(_mm_stream_kernel, act=act, w_transposed=w_transposed),
        grid=grid,
        in_specs=[aspec, wspec],
        out_specs=ospec,
        out_shape=jax.ShapeDtypeStruct((m, n), out_dtype),
        compiler_params=_params("parallel", "parallel"),
    )(a, w)


def _write_gdn_gates(w_ref, wb_ref, a, alog_ref, dtb_ref, o_ref, n_heads):
    @pl.when(pl.program_id(0) == 0)
    def _():
        wb_ref[...] = w_ref[...].astype(BF16)

    logits = lax.dot_general(wb_ref[...], a, (((1,), (1,)), ((), ())),
                             preferred_element_type=F32)
    o_ref[:n_heads, :] = _sigmoid(logits[:n_heads, :])
    dl = logits[n_heads:2 * n_heads, :] + dtb_ref[...]
    softplus = jnp.maximum(dl, 0.0) + jnp.log1p(jnp.exp(-jnp.abs(dl)))
    s = -jnp.exp(alog_ref[...]) * softplus
    pos = lax.broadcasted_iota(jnp.int32, s.shape, 1) & (GDN_BLOCK - 1)
    shift = 1
    while shift < GDN_BLOCK:
        s = s + jnp.where(pos >= shift, pltpu.roll(s, shift, axis=1), 0.0)
        shift *= 2
    o_ref[n_heads:, :] = s


def _gdn_gates_kernel(a_ref, w_ref, alog_ref, dtb_ref, o_ref, wb_ref, *, n_heads):
    _write_gdn_gates(w_ref, wb_ref, a_ref[...], alog_ref, dtb_ref, o_ref, n_heads)


def _prenorm_gates_kernel(x_ref, g_ref, w_ref, alog_ref, dtb_ref, h_ref, o_ref, wb_ref, *, n_heads):
    h = _rms(x_ref[...], g_ref[...]).astype(h_ref.dtype)
    h_ref[...] = h
    _write_gdn_gates(w_ref, wb_ref, h, alog_ref, dtb_ref, o_ref, n_heads)


def prenorm_with_gates(x, g, w_t, layer, row0, a_log, dt_bias, *, tr=512):
    t, d = x.shape
    n_heads = a_log.shape[0]
    assert row0 % LANES == 0 and 2 * n_heads <= LANES
    col = pl.BlockSpec((n_heads, 1), lambda i: (0, 0))
    return pl.pallas_call(
        functools.partial(_prenorm_gates_kernel, n_heads=n_heads),
        grid=(t // tr,),
        in_specs=[pl.BlockSpec((tr, d), lambda i: (i, 0)),
                  pl.BlockSpec((1, d), lambda i: (0, 0)),
                  pl.BlockSpec((None, LANES, d), lambda i: (layer, row0 // LANES, 0)),
                  col, col],
        out_specs=[pl.BlockSpec((tr, d), lambda i: (i, 0)),
                   pl.BlockSpec((2 * n_heads, tr), lambda i: (0, i))],
        out_shape=[jax.ShapeDtypeStruct((t, d), BF16),
                   jax.ShapeDtypeStruct((2 * n_heads, t), F32)],
        scratch_shapes=[pltpu.VMEM((LANES, d), BF16)],
        compiler_params=_params("arbitrary"),
    )(x, g.reshape(1, d), w_t, a_log.astype(F32).reshape(n_heads, 1),
      dt_bias.astype(F32).reshape(n_heads, 1))


def gdn_gates(a, w_t, layer, row0, a_log, dt_bias, *, tn):
    n_heads = a_log.shape[0]
    m, kdim = a.shape
    assert row0 % LANES == 0 and 2 * n_heads <= LANES
    col = pl.BlockSpec((n_heads, 1), lambda i: (0, 0))
    return pl.pallas_call(
        functools.partial(_gdn_gates_kernel, n_heads=n_heads),
        grid=(m // tn,),
        in_specs=[pl.BlockSpec((tn, kdim), lambda i: (i, 0)),
                  pl.BlockSpec((None, LANES, kdim), lambda i: (layer, row0 // LANES, 0)),
                  col, col],
        out_specs=pl.BlockSpec((2 * n_heads, tn), lambda i: (0, i)),
        out_shape=jax.ShapeDtypeStruct((2 * n_heads, m), F32),
        scratch_shapes=[pltpu.VMEM((LANES, kdim), BF16)],
        compiler_params=_params("arbitrary"),
    )(a, w_t, a_log.astype(F32).reshape(n_heads, 1), dt_bias.astype(F32).reshape(n_heads, 1))


def _ffn_up_kernel(a_ref, wg_ref, wu_ref, wd_ref, o_ref, wd_bf_ref):
    @pl.when(pl.program_id(0) == 0)
    def _():
        slab = wd_bf_ref.shape[2]
        for p in range(wd_bf_ref.shape[0]):
            wd_bf_ref[p] = wd_ref[:, p * slab:(p + 1) * slab].astype(BF16)

    a = a_ref[...]
    g = jnp.dot(a, wg_ref[...].astype(BF16), preferred_element_type=F32)
    u = jnp.dot(a, wu_ref[...].astype(BF16), preferred_element_type=F32)
    o_ref[...] = (_silu(g) * u).astype(o_ref.dtype)


def ffn_up(a, wg, wu, wd, layer, *, tm, tn, wd_slab):
    m, kdim = a.shape
    n = wg.shape[2]
    d_out = wd.shape[2]
    assert m % tm == 0 and n % tn == 0 and d_out % wd_slab == 0
    nb = n // tn
    n_slabs = d_out // wd_slab
    wspec = pl.BlockSpec((None, kdim, tn), lambda i, j: (layer, 0, j))
    wd_rows = lambda i, j: jnp.where(i == 0, j, nb - 1)
    return pl.pallas_call(
        _ffn_up_kernel,
        grid=(m // tm, nb),
        in_specs=[pl.BlockSpec((tm, kdim), lambda i, j: (i, 0), pipeline_mode=pl.Buffered(1)),
                  wspec, wspec,
                  pl.BlockSpec((None, tn, d_out), lambda i, j: (layer, wd_rows(i, j), 0))],
        out_specs=[pl.BlockSpec((tm, tn), lambda i, j: (i, j)),
                   pl.BlockSpec((n_slabs, tn, wd_slab), lambda i, j: (0, wd_rows(i, j), 0))],
        out_shape=[jax.ShapeDtypeStruct((m, n), BF16),
                   jax.ShapeDtypeStruct((n_slabs, n, wd_slab), BF16)],
        compiler_params=_params("arbitrary", "arbitrary"),
    )(a, wg, wu, wd)


def _pool_kernel(halo_ref, x_ref, w_ref, scale_ref, o_ref, *, tb, seq, group):
    start = (pl.program_id(0) * tb) % seq
    keep_halo = (start > 0).astype(F32)
    pos = start + 1 + lax.broadcasted_iota(jnp.int32, (tb, 1), 0)
    for gi, win in enumerate(POOL_WINDOWS):
        cols = slice(gi * group, (gi + 1) * group)
        x = x_ref[:, cols].astype(F32)
        s = jnp.concatenate([halo_ref[:, cols].astype(F32) * keep_halo, x], axis=0)
        shift = 1
        while shift < win:
            s = s + pltpu.roll(s, shift, axis=0)
            shift *= 2
        cnt = jnp.minimum(pos, win).astype(F32)
        y = s[POOL_HALO:, :] / cnt - x
        r = jnp.dot(y.astype(BF16), w_ref[gi], preferred_element_type=F32)
        o_ref[:, cols] = (r * scale_ref[:, cols]).astype(o_ref.dtype)
    width = len(POOL_WINDOWS) * group
    o_ref[:, width:] = jnp.zeros((tb, o_ref.shape[1] - width), o_ref.dtype)


def pool_mixer(z, w_grp, scale, *, seq, out_width, tb=512):
    t = z.shape[0]
    n_g, group, _ = w_grp.shape
    width = n_g * group
    hb = tb // POOL_HALO
    return pl.pallas_call(
        functools.partial(_pool_kernel, tb=tb, seq=seq, group=group),
        grid=(t // tb,),
        in_specs=[pl.BlockSpec((POOL_HALO, width), lambda i: (jnp.maximum(i * hb - 1, 0), 0)),
                  pl.BlockSpec((tb, width), lambda i: (i, 0)),
                  pl.BlockSpec((n_g, group, group), lambda i: (0, 0, 0)),
                  pl.BlockSpec((1, width), lambda i: (0, 0))],
        out_specs=pl.BlockSpec((tb, out_width), lambda i: (i, 0)),
        out_shape=jax.ShapeDtypeStruct((t, out_width), BF16),
        compiler_params=_params("parallel"),
    )(z, z, w_grp, scale.reshape(1, width))


def _gdn_kernel(q_ref, k_ref, v_ref, gate_ref, gates_ref, cq_ref, ck_ref, cv_ref, nw_ref,
                y_hbm_ref, o_ref, halo_ref, state_ref, *, hps, n_heads):
    del y_hbm_ref
    c = GDN_BLOCK
    d = GDN_HEAD_DIM
    halo = SUBLANES
    heads = range(hps)
    t = pl.program_id(2)
    h0 = pl.program_id(1) * hps

    @pl.when(t == 0)
    def _():
        halo_ref[...] = jnp.zeros_like(halo_ref)
        state_ref[...] = jnp.zeros_like(state_ref)

    width = hps * d
    sub = lax.broadcasted_iota(jnp.int32, (1, halo, width), 1)

    def conv_silu(x_ref, idx, cw_ref):
        cur = x_ref[...].astype(F32)
        tiles = jnp.concatenate([halo_ref[idx], cur], axis=0).reshape(c // halo + 1, halo, width)
        w = cw_ref[...]
        acc = cur * w[CONV_WIDTH - 1:CONV_WIDTH, :]
        for back in range(1, CONV_WIDTH):
            rot = pltpu.roll(tiles, back, axis=1)
            tap = jnp.where(sub < back, rot[:-1], rot[1:]).reshape(c, width)
            acc = acc + tap * w[CONV_WIDTH - 1 - back:CONV_WIDTH - back, :]
        halo_ref[idx] = cur[c - halo:, :]
        return _silu(acc)

    qs = conv_silu(q_ref, 0, cq_ref)
    ks = conv_silu(k_ref, 1, ck_ref)
    vs = conv_silu(v_ref, 2, cv_ref)

    ii = lax.broadcasted_iota(jnp.int32, (c, c), 0)
    jj = lax.broadcasted_iota(jnp.int32, (c, c), 1)
    causal = ii >= jj
    strict = ii > jj

    def bf(a):
        return a.astype(BF16)

    def mm(a, b):
        return jnp.dot(a, b, preferred_element_type=F32)

    def mm_nt(a, b):
        return lax.dot_general(a, b, (((1,), (1,)), ((), ())), preferred_element_type=F32)

    def l2norm(x):
        return x * lax.rsqrt(jnp.sum(x * x, axis=-1, keepdims=True) + EPS)

    beta_rows = gates_ref[pl.ds(h0, hps), :]
    gc_rows = gates_ref[pl.ds(n_heads + h0, hps), :]

    decay, gc_last, kk, qk, rhs, qd_bf, kd_t_bf = [], [], [], [], [], [], []
    for j in heads:
        q = l2norm(qs[:, j * d:(j + 1) * d]) * (d ** -0.5)
        k = l2norm(ks[:, j * d:(j + 1) * d])
        v = vs[:, j * d:(j + 1) * d]
        beta = jnp.broadcast_to(beta_rows[j:j + 1, :], (c, c)).T
        gc_t = jnp.broadcast_to(gc_rows[j:j + 1, :], (c, c))
        gc = gc_t.T
        gamma = jnp.exp(gc)
        last = gc[c - 1:c, :]
        kb = k * beta
        k_bf = bf(k)
        kk.append(mm_nt(bf(kb), k_bf))
        qk.append(mm_nt(bf(q), k_bf))
        decay.append(jnp.exp(jnp.where(causal, gc - gc_t, -jnp.inf)))
        gc_last.append(last)
        rhs.append(jnp.concatenate([kb * gamma, v * beta], axis=1))
        qd_bf.append(bf(q * gamma))
        kd_t_bf.append(bf((k * jnp.exp(last - gc)).T))
    lmat = [jnp.where(strict, kk[j] * decay[j], 0.0) for j in heads]
    attn_bf = [bf(qk[j] * decay[j]) for j in heads]

    n = [-m for m in lmat]
    p_bf = [bf(m) for m in lmat]
    span = 2
    while span < c:
        p = [mm(m, m) for m in p_bf]
        p_bf = [bf(m) for m in p]
        n = [n[j] + p[j] + mm(bf(n[j]), p_bf[j]) for j in heads]
        span *= 2
    sol = [rhs[j] + mm(bf(n[j]), bf(rhs[j])) for j in heads]

    state = [state_ref[j] for j in heads]
    state_bf = [bf(m) for m in state]
    v_new = [sol[j][:, d:] - mm(bf(sol[j][:, :d]), state_bf[j]) for j in heads]
    v_new_bf = [bf(m) for m in v_new]
    o = [mm(qd_bf[j], state_bf[j]) + mm(attn_bf[j], v_new_bf[j]) for j in heads]
    for j in heads:
        state_ref[j] = jnp.exp(gc_last[j]) * state[j] + mm(kd_t_bf[j], v_new_bf[j])
    for j in heads:
        gate = gate_ref[:, j * d:(j + 1) * d].astype(F32)
        o_ref[:, j * d:(j + 1) * d] = (_rms(o[j], nw_ref[...]) * _silu(gate)).astype(o_ref.dtype)


def gated_deltanet(z, gates, conv_w, norm_w, y_mix, *, batch, seq, col0, width):
    d = GDN_HEAD_DIM
    c = GDN_BLOCK
    hps = GDN_HEADS_PER_STEP
    n_heads = width // d
    bw = hps * d
    nblk = seq // c
    cb0 = col0 // bw
    wb = width // bw
    ocb0 = (y_mix.shape[1] - width) // bw

    def zspec(part):
        return pl.BlockSpec((c, bw), lambda b, h, t: (b * nblk + t, cb0 + part * wb + h))

    def cspec(part):
        return pl.BlockSpec((CONV_WIDTH, bw), lambda b, h, t: (0, part * wb + h))

    return pl.pallas_call(
        functools.partial(_gdn_kernel, hps=hps, n_heads=n_heads),
        grid=(batch, n_heads // hps, nblk),
        in_specs=[zspec(0), zspec(1), zspec(2), zspec(3),
                  pl.BlockSpec((2 * n_heads, c), lambda b, h, t: (0, b * nblk + t)),
                  cspec(0), cspec(1), cspec(2),
                  pl.BlockSpec((1, d), lambda b, h, t: (0, 0)),
                  pl.BlockSpec(memory_space=pl.ANY)],
        out_specs=pl.BlockSpec((c, bw), lambda b, h, t: (b * nblk + t, ocb0 + h)),
        out_shape=jax.ShapeDtypeStruct(y_mix.shape, y_mix.dtype),
        input_output_aliases={9: 0},
        scratch_shapes=[pltpu.VMEM((3, SUBLANES, bw), F32), pltpu.VMEM((hps, d, d), F32)],
        compiler_params=_params("parallel", "parallel", "arbitrary"),
    )(z, z, z, z, gates, conv_w, conv_w, conv_w, norm_w.reshape(1, d), y_mix)


def _sgu_kernel(u_ref, v_ref, lng_ref, lnb_ref, ws_ref, bs_ref, o_ref, *, group):
    v = v_ref[...].astype(F32)
    mu = jnp.mean(v, axis=-1, keepdims=True)
    xc = v - mu
    vn = xc * lax.rsqrt(jnp.mean(xc * xc, axis=-1, keepdims=True) + EPS) * lng_ref[...] + lnb_ref[...]
    vb = vn.astype(BF16)
    ii = lax.broadcasted_iota(jnp.int32, (SGU_LEN, SGU_LEN), 0)
    jj = lax.broadcasted_iota(jnp.int32, (SGU_LEN, SGU_LEN), 1)
    chunk_bits = SGU_CHUNK.bit_length() - 1
    mask = (jj >> chunk_bits) <= (ii >> chunk_bits)
    for g in range(SGU_GROUPS):
        cols = slice(g * group, (g + 1) * group)
        ws = jnp.where(mask, ws_ref[g], 0.0).astype(BF16)
        for blk in range(SGU_BLOCKS_PER_STEP):
            rows = slice(blk * SGU_LEN, (blk + 1) * SGU_LEN)
            sv = jnp.dot(ws, vb[rows, cols], preferred_element_type=F32) + bs_ref[:, g:g + 1]
            o_ref[rows, cols] = (u_ref[rows, cols].astype(F32) * sv).astype(o_ref.dtype)


def spatial_gating(z, ln_g, ln_b, w_s, b_s):
    t, w2 = z.shape
    width = w2 // 2
    group = width // SGU_GROUPS
    tr = SGU_BLOCKS_PER_STEP * SGU_LEN
    assert t % tr == 0
    vec = pl.BlockSpec((1, width), lambda i: (0, 0))
    return pl.pallas_call(
        functools.partial(_sgu_kernel, group=group),
        grid=(t // tr,),
        in_specs=[pl.BlockSpec((tr, width), lambda i: (i, 0)),
                  pl.BlockSpec((tr, width), lambda i: (i, 1)),
                  vec, vec,
                  pl.BlockSpec((SGU_GROUPS, SGU_LEN, SGU_LEN), lambda i: (0, 0, 0)),
                  pl.BlockSpec((SGU_LEN, SGU_GROUPS), lambda i: (0, 0))],
        out_specs=pl.BlockSpec((tr, width), lambda i: (i, 0)),
        out_shape=jax.ShapeDtypeStruct((t, width), BF16),
        compiler_params=_params("parallel"),
    )(z, z, ln_g.reshape(1, width), ln_b.reshape(1, width), w_s, b_s.T)


def _gate_row0(pool_w, a_log):
    n_g, group, _ = pool_w.shape
    return n_g * group + 4 * a_log.shape[0] * GDN_HEAD_DIM


def _mix_pool_gdn(h, gates, i, w_in_t, pool_w, pool_scale, conv_w, a_log, dt_bias, norm_w, w_out, *,
                  batch, seq):
    n_g, group, _ = pool_w.shape
    pool_width = n_g * group
    gdn_width = a_log.shape[0] * GDN_HEAD_DIM
    main = _gate_row0(pool_w, a_log)
    z = matmul_wstream(h, w_in_t, i, n_cols=main, tm=1024, tn=1024, w_resident=True, out_dtype=BF16,
                       w_transposed=True)
    if gates is None:
        gates = gdn_gates(h, w_in_t, i, main, a_log, dt_bias, tn=1024)
    y = pool_mixer(z, pool_w.astype(BF16), pool_scale, seq=seq, out_width=pool_width + gdn_width)
    y = gated_deltanet(z, gates, conv_w, norm_w, y, batch=batch, seq=seq, col0=pool_width,
                       width=gdn_width)
    return matmul_wstream(y, w_out, i, tm=1024, tn=1024, w_resident=True, out_dtype=BF16)


def _mix_sgu(h, i, w_in, ln_g, ln_b, w_s, b_s, w_out):
    z = matmul_wstream(h, w_in, i, tm=1024, tn=1024, w_resident=True, out_dtype=BF16, act="gelu")
    s = spatial_gating(z, ln_g, ln_b, w_s, b_s)
    return matmul_wstream(s, w_out, i, tm=1024, tn=1024, w_resident=True, out_dtype=BF16)


def _swiglu(h, layer, w_gate, w_up, w_down):
    act, wd = ffn_up(h, w_gate, w_up, w_down, layer, tm=2048, tn=256, wd_slab=512)
    return matmul_wstream(act, wd, None, tm=512, tn=512, out_dtype=BF16)


def kernel(x, norm_mix_pre, norm_mix_post, norm_ffn_pre, norm_ffn_post, ab_w_in, pool_w, pool_scale, gdn_conv, gdn_a_log, gdn_dt_bias, gdn_norm, ab_w_out, sgu_w_in, sgu_ln_g, sgu_ln_b, sgu_w_s, sgu_b_s, sgu_w_out, ffn_w_gate, ffn_w_up, ffn_w_down):
    batch, seq, d_model = x.shape
    depth = norm_mix_pre.shape[0]
    xt = x.reshape(batch * seq, d_model)
    ab_w_in_t = jnp.swapaxes(ab_w_in, 1, 2)
    h, gates = prenorm_with_gates(xt, norm_mix_pre[0], ab_w_in_t, 0,
                                  _gate_row0(pool_w[0], gdn_a_log[0]), gdn_a_log[0], gdn_dt_bias[0])
    for layer in range(depth):
        i = layer // 2
        if layer % 2 == 0:
            y = _mix_pool_gdn(h, gates if layer == 0 else None, i, ab_w_in_t, pool_w[i],
                              pool_scale[i], gdn_conv[i], gdn_a_log[i], gdn_dt_bias[i],
                              gdn_norm[i], ab_w_out, batch=batch, seq=seq)
        else:
            y = _mix_sgu(h, i, sgu_w_in, sgu_ln_g[i], sgu_ln_b[i], sgu_w_s[i], sgu_b_s[i], sgu_w_out)
        xt, h = postnorm_residual(y, xt, norm_mix_post[layer], norm_ffn_pre[layer])
        y = _swiglu(h, layer, ffn_w_gate, ffn_w_up, ffn_w_down)
        nxt = norm_mix_pre[layer + 1] if layer + 1 < depth else None
        xt, h = postnorm_residual(y, xt, norm_ffn_post[layer], nxt)
    return xt.reshape(batch, seq, d_model)
```
